```python
import math
import jax, jax.numpy as jnp
from jax import lax
import numpy as np

D_MODEL = 1024
BATCH = 8
SEQ = 2048
DEPTH = 1
DEC_BATCH = 1
DEC_SEQ = 16384
PAST_LEN = 128

HEAD_DIM = 64
N_HEADS = 8
N_KV_HEADS = 2
WINDOW = 128
BLOCK = 128
N_BUCKETS = 32
MAX_DISTANCE = 128
SGU_WIDTH = 512
SGU_GROUPS = 4
SGU_CHUNK = 128
MEM_LEN = 256
MEM_HEADS = 4
MEM_HEAD_DIM = 128
MEM_WIDTH = MEM_HEADS * MEM_HEAD_DIM
N_BRANCHES = 3
D_FF = 2816
CONV_WIDTH = 3
EPS = 1e-6

Q_W = N_HEADS * HEAD_DIM
KV_W = N_KV_HEADS * HEAD_DIM
IN_COLS = Q_W + 2 * KV_W + 2 * SGU_WIDTH + MEM_WIDTH + N_BRANCHES * D_MODEL
SPLITS = [int(s) for s in np.cumsum([Q_W, KV_W, KV_W, 2 * SGU_WIDTH, MEM_WIDTH])]
NEG = -1e30

kernel_name = "hybrid_gated_window_gmlp_memory_encoder"


def rmsnorm(x, g):
    xf = x.astype(jnp.float32)
    y = xf * lax.rsqrt(jnp.mean(xf * xf, axis=-1, keepdims=True) + EPS)
    return (y * g.astype(jnp.float32)).astype(x.dtype)


def t5_bucket(rel):
    nb = N_BUCKETS // 2
    max_exact = nb // 2
    ret = (rel > 0).astype(jnp.int32) * nb
    n = jnp.abs(rel)
    nf = jnp.maximum(n, 1).astype(jnp.float32)
    large = max_exact + (jnp.log(nf / max_exact) / math.log(MAX_DISTANCE / max_exact)
                         * (nb - max_exact)).astype(jnp.int32)
    large = jnp.minimum(large, nb - 1)
    return ret + jnp.where(n < max_exact, n, large)


def windowed_gqa(q, k, v, sink, rel_bias):
    B, S = q.shape[0], q.shape[1]
    nb = S // BLOCK
    G = N_HEADS // N_KV_HEADS
    qb = q.reshape(B, nb, BLOCK, N_KV_HEADS, G, HEAD_DIM)
    pad = ((0, 0), (BLOCK, BLOCK), (0, 0), (0, 0))
    kp = jnp.pad(k, pad)
    vp = jnp.pad(v, pad)

    def bands(t):
        return jnp.concatenate(
            [t[:, i * BLOCK:i * BLOCK + S].reshape(B, nb, BLOCK, N_KV_HEADS, HEAD_DIM) for i in range(3)],
            axis=2)

    kb = bands(kp)
    vb = bands(vp)
    s = jnp.einsum('bnqkgd,bnjkd->bnkgqj', qb, kb).astype(jnp.float32) * (HEAD_DIM ** -0.5)
    qi = jnp.arange(BLOCK)[:, None]
    kj = jnp.arange(3 * BLOCK)[None, :]
    rel = (kj - BLOCK) - qi
    bias = rel_bias[t5_bucket(rel)].astype(jnp.float32)
    bias = bias.transpose(2, 0, 1).reshape(N_KV_HEADS, G, BLOCK, 3 * BLOCK)
    kpos = (jnp.arange(nb)[:, None] - 1) * BLOCK + jnp.arange(3 * BLOCK)[None, :]
    valid = (jnp.abs(rel) <= WINDOW)[None] & ((kpos >= 0) & (kpos < S))[:, None, :]
    s = jnp.where(valid[None, :, None, None], s + bias[None, None], NEG)
    snk = sink.astype(jnp.float32).reshape(N_KV_HEADS, G)[None, None, :, :, None, None]
    m = jnp.maximum(jnp.max(s, axis=-1, keepdims=True), snk)
    p = jnp.exp(s - m)
    den = jnp.sum(p, axis=-1, keepdims=True) + jnp.exp(snk - m)
    p = (p / den).astype(v.dtype)
    o = jnp.einsum('bnkgqj,bnjkd->bnqkgd', p, vb)
    return o.reshape(B, S, Q_W)


def spatial_gating(z, w_s, b_s, ln_g, ln_b):
    B, S = z.shape[0], z.shape[1]
    u, v = jnp.split(z, 2, axis=-1)
    vf = v.astype(jnp.float32)
    mu = jnp.mean(vf, axis=-1, keepdims=True)
    var = jnp.mean(jnp.square(vf - mu), axis=-1, keepdims=True)
    v = ((vf - mu) * lax.rsqrt(var + EPS) * ln_g.astype(jnp.float32)
         + ln_b.astype(jnp.float32)).astype(z.dtype)
    nc = S // SGU_CHUNK
    cg = SGU_WIDTH // SGU_GROUPS
    vc = v.reshape(B, nc, SGU_CHUNK, SGU_GROUPS, cg)
    sv = jnp.einsum('gij,bnjgc->bnigc', w_s, vc) + b_s.T[None, None, :, :, None]
    return u * sv.reshape(B, S, SGU_WIDTH)


def memory_attention(qm, mem_n, w_mem_kv):
    B, S = qm.shape[0], qm.shape[1]
    M = mem_n.shape[1]
    q = qm.reshape(B, S, MEM_HEADS, MEM_HEAD_DIM)
    kv = mem_n @ w_mem_kv
    k, v = jnp.split(kv, 2, axis=-1)
    k = k.reshape(B, M, MEM_HEADS, MEM_HEAD_DIM)
    v = v.reshape(B, M, MEM_HEADS, MEM_HEAD_DIM)
    s = jnp.einsum('bshd,bmhd->bhsm', q, k).astype(jnp.float32) * (MEM_HEAD_DIM ** -0.5)
    p = jax.nn.softmax(s, axis=-1).astype(v.dtype)
    o = jnp.einsum('bhsm,bmhd->bshd', p, v)
    return o.reshape(B, S, MEM_WIDTH)


def conv_ffn(h, w_up, conv_w, conv_b, w_down):
    S = h.shape[1]
    a = h @ w_up
    ap = jnp.pad(a, ((0, 0), (1, 1), (0, 0)))
    a = ap[:, :S] * conv_w[0] + ap[:, 1:S + 1] * conv_w[1] + ap[:, 2:] * conv_w[2] + conv_b
    gate, up = jnp.split(a, 2, axis=-1)
    return (jax.nn.gelu(gate) * up) @ w_down


def encoder(x, mem, rel_bias, g_mix, w_in, attn_sink, sgu_w, sgu_b, sgu_ln_g, sgu_ln_b,
            g_mem, w_mem_kv, w_br_attn, w_br_sgu, w_br_mem, w_out, g_ffn, w_up, conv_w,
            conv_b, w_down, g_final):
    B, S = x.shape[0], x.shape[1]
    for l in range(DEPTH):
        h = rmsnorm(x, g_mix[l])
        proj = h @ w_in[l]
        q, k, v, z, qm, gl = jnp.split(proj, SPLITS, axis=-1)
        attn = windowed_gqa(q.reshape(B, S, N_HEADS, HEAD_DIM),
                            k.reshape(B, S, N_KV_HEADS, HEAD_DIM),
                            v.reshape(B, S, N_KV_HEADS, HEAD_DIM),
                            attn_sink[l], rel_bias)
        sgu = spatial_gating(jax.nn.gelu(z), sgu_w[l], sgu_b[l], sgu_ln_g[l], sgu_ln_b[l])
        mo = memory_attention(qm, rmsnorm(mem, g_mem[l]), w_mem_kv[l])
        g = jax.nn.sigmoid(gl.astype(jnp.float32)).astype(x.dtype).reshape(B, S, N_BRANCHES, D_MODEL)
        merged = (g[:, :, 0] * (attn @ w_br_attn[l]) + g[:, :, 1] * (sgu @ w_br_sgu[l])
                  + g[:, :, 2] * (mo @ w_br_mem[l]))
        x = x + merged @ w_out[l]
        x = x + conv_ffn(rmsnorm(x, g_ffn[l]), w_up[l], conv_w[l], conv_b[l], w_down[l])
    return rmsnorm(x, g_final)


def setup_inputs(seed: int = 0) -> dict:
    key = jax.random.key(seed)
    ks = jax.random.split(key, 32)
    f32 = jnp.float32

    def nrm(k, shape, scale):
        return jax.random.normal(k, shape, f32) * scale

    def gain(k, shape):
        return 1.0 + 0.02 * jax.random.normal(k, shape, f32)

    L = DEPTH
    return {
        "x_prompt": nrm(ks[0], (BATCH, SEQ, D_MODEL), 1.0),
        "x_sample": nrm(ks[1], (DEC_BATCH, DEC_SEQ, D_MODEL), 1.0),
        "mem_prompt": nrm(ks[2], (BATCH, MEM_LEN, D_MODEL), 1.0),
        "mem_sample": nrm(ks[3], (DEC_BATCH, MEM_LEN, D_MODEL), 1.0),
        "rel_bias": nrm(ks[4], (N_BUCKETS, N_HEADS), 0.5),
        "g_mix": gain(ks[5], (L, D_MODEL)),
        "w_in": nrm(ks[6], (L, D_MODEL, IN_COLS), D_MODEL ** -0.5),
        "attn_sink": nrm(ks[7], (L, N_HEADS), 0.5),
        "sgu_w": nrm(ks[8], (L, SGU_GROUPS, SGU_CHUNK, SGU_CHUNK), SGU_CHUNK ** -0.5),
        "sgu_b": gain(ks[9], (L, SGU_GROUPS, SGU_CHUNK)),
        "sgu_ln_g": gain(ks[10], (L, SGU_WIDTH)),
        "sgu_ln_b": nrm(ks[11], (L, SGU_WIDTH), 0.02),
        "g_mem": gain(ks[12], (L, D_MODEL)),
        "w_mem_kv": nrm(ks[13], (L, D_MODEL, 2 * MEM_WIDTH), D_MODEL ** -0.5),
        "w_br_attn": nrm(ks[14], (L, Q_W, D_MODEL), Q_W ** -0.5),
        "w_br_sgu": nrm(ks[15], (L, SGU_WIDTH, D_MODEL), SGU_WIDTH ** -0.5),
        "w_br_mem": nrm(ks[16], (L, MEM_WIDTH, D_MODEL), MEM_WIDTH ** -0.5),
        "w_out": nrm(ks[17], (L, D_MODEL, D_MODEL), D_MODEL ** -0.5),
        "g_ffn": gain(ks[18], (L, D_MODEL)),
        "w_up": nrm(ks[19], (L, D_MODEL, 2 * D_FF), D_MODEL ** -0.5),
        "conv_w": nrm(ks[20], (L, CONV_WIDTH, 2 * D_FF), CONV_WIDTH ** -0.5),
        "conv_b": nrm(ks[21], (L, 2 * D_FF), 0.02),
        "w_down": nrm(ks[22], (L, D_FF, D_MODEL), D_FF ** -0.5),
        "g_final": gain(ks[23], (D_MODEL,)),
    }


def reference(x_prompt, x_sample, mem_prompt, mem_sample, rel_bias, g_mix, w_in, attn_sink,
              sgu_w, sgu_b, sgu_ln_g, sgu_ln_b, g_mem, w_mem_kv, w_br_attn, w_br_sgu, w_br_mem,
              w_out, g_ffn, w_up, conv_w, conv_b, w_down, g_final):
    y_prompt = encoder(x_prompt, mem_prompt, rel_bias, g_mix, w_in, attn_sink, sgu_w, sgu_b,
                       sgu_ln_g, sgu_ln_b, g_mem, w_mem_kv, w_br_attn, w_br_sgu, w_br_mem,
                       w_out, g_ffn, w_up, conv_w, conv_b, w_down, g_final)
    y_sample = encoder(x_sample, mem_sample, rel_bias, g_mix, w_in, attn_sink, sgu_w, sgu_b,
                       sgu_ln_g, sgu_ln_b, g_mem, w_mem_kv, w_br_attn, w_br_sgu, w_br_mem,
                       w_out, g_ffn, w_up, conv_w, conv_b, w_down, g_final)
    return (y_prompt, y_sample)
```

```python
import functools
import math

import jax
import jax.numpy as jnp
from jax import lax
from jax.experimental import pallas as pl
from jax.experimental.pallas import tpu as pltpu

D_MODEL = 1024
HEAD_DIM = 64
N_HEADS = 8
N_KV_HEADS = 2
WINDOW = 128
BLOCK = 128
N_BUCKETS = 32
MAX_DISTANCE = 128
SGU_WIDTH = 512
SGU_GROUPS = 4
SGU_CHUNK = 128
MEM_HEADS = 4
MEM_HEAD_DIM = 128
MEM_WIDTH = MEM_HEADS * MEM_HEAD_DIM
D_FF = 2816
EPS = 1e-6
NEG = -1e30

Q_W = N_HEADS * HEAD_DIM
KV_W = N_KV_HEADS * HEAD_DIM
OFF_K = Q_W
OFF_Z = Q_W + 2 * KV_W
OFF_QM = OFF_Z + 2 * SGU_WIDTH
OFF_GL = OFF_QM + MEM_WIDTH

LANES = 128
BF16_ROWS = 16
VMEM_LIMIT = 56 * 1024 * 1024

TILE = 512
FFN_COLS = 256

BF16 = jnp.bfloat16
F32 = jnp.float32


def _rms(x, g):
    return x * lax.rsqrt(jnp.mean(x * x, axis=-1, keepdims=True) + EPS) * g


def _dot(a, b):
    return jnp.dot(a, b, preferred_element_type=F32)


def _const_spec(shape):
    nd = len(shape)
    return pl.BlockSpec(shape, lambda *_: (0,) * nd, pipeline_mode=pl.Buffered(1))


def _mem_kv_kernel(mem_ref, g_ref, w_ref, kt_ref, v_ref):
    h = _rms(mem_ref[0], g_ref[...]).astype(BF16)
    kv = _dot(h, w_ref[...])
    k = kv[:, :MEM_WIDTH] * (MEM_HEAD_DIM ** -0.5)
    kt_ref[0] = k.T.astype(BF16)
    v_ref[0] = kv[:, MEM_WIDTH:].astype(BF16)


def _mem_kv(mem, g_mem, w_mem_kv):
    B, M, D = mem.shape
    return pl.pallas_call(
        _mem_kv_kernel,
        grid=(B,),
        in_specs=[
            pl.BlockSpec((1, M, D), lambda b: (b, 0, 0)),
            _const_spec((1, D)),
            _const_spec((D, 2 * MEM_WIDTH)),
        ],
        out_specs=[
            pl.BlockSpec((1, MEM_WIDTH, M), lambda b: (b, 0, 0)),
            pl.BlockSpec((1, M, MEM_WIDTH), lambda b: (b, 0, 0)),
        ],
        out_shape=[
            jax.ShapeDtypeStruct((B, MEM_WIDTH, M), BF16),
            jax.ShapeDtypeStruct((B, M, MEM_WIDTH), BF16),
        ],
        compiler_params=pltpu.CompilerParams(
            dimension_semantics=("arbitrary",), vmem_limit_bytes=VMEM_LIMIT),
        name="mem_kv",
    )(mem, g_mem, w_mem_kv)


def _mixer_kernel(x_ref, xp_ref, xn_ref, ktm_ref, vm_ref, bucket_ref, relb_ref, sink_ref,
                  g_ref, wq_ref, wkv_ref, wz_ref, wqm_ref, wgl_ref,
                  sguw_ref, sgub_ref, lng_ref, lnb_ref,
                  wba_ref, wbs_ref, wbm_ref, wo_ref,
                  o_ref,
                  bias_ref, hext_ref, kvar_ref, vvar_ref, q_ref, attn_ref,
                  u_ref, vln_ref, sgu_ref, mo_ref):
    ts = x_ref.shape[1]
    nq = ts // BLOCK
    i = pl.program_id(1)
    nt = pl.num_programs(1)
    first_step = jnp.logical_and(pl.program_id(0) == 0, i == 0)

    @pl.when(first_step)
    def _():
        bk = bucket_ref[...]
        qi = lax.broadcasted_iota(jnp.int32, (BLOCK, 3 * BLOCK), 0)
        kj = lax.broadcasted_iota(jnp.int32, (BLOCK, 3 * BLOCK), 1)
        in_window = jnp.abs(kj - BLOCK - qi) <= WINDOW
        for hd in range(N_HEADS):
            acc = jnp.zeros((BLOCK, 3 * BLOCK), F32)
            for b in range(N_BUCKETS):
                acc = jnp.where(bk == b, relb_ref[b, hd], acc)
            base = jnp.where(in_window, acc, NEG)
            kv, rem = divmod(hd, N_HEADS // N_KV_HEADS)
            half, slot = divmod(rem, 2)
            c = kv * 2 + slot
            rows = pl.ds(half * BLOCK, BLOCK)
            bias_ref[0, c, rows, :] = base
            bias_ref[1, c, rows, :] = jnp.where(kj < BLOCK, NEG, base)
            bias_ref[2, c, rows, :] = jnp.where(kj >= 2 * BLOCK, NEG, base)

    g = g_ref[...]
    hext_ref[0:BLOCK] = _rms(xp_ref[0], g).astype(BF16)
    hext_ref[BLOCK:BLOCK + ts] = _rms(x_ref[0], g).astype(BF16)
    hext_ref[BLOCK + ts:] = _rms(xn_ref[0], g).astype(BF16)
    h = hext_ref[BLOCK:BLOCK + ts]

    w = ts + 2 * BLOCK
    kv_all = _dot(hext_ref[...], wkv_ref[...])
    kt = kv_all[:, :KV_W].T.astype(BF16)
    v = kv_all[:, KV_W:]
    vswap = pltpu.roll(v, HEAD_DIM, axis=1)
    left = lax.broadcasted_iota(jnp.int32, (w, KV_W), 1) < HEAD_DIM
    zrow = jnp.zeros((HEAD_DIM, w), BF16)
    for kvh in range(N_KV_HEADS):
        kk = kt[kvh * HEAD_DIM:(kvh + 1) * HEAD_DIM]
        kvar_ref[kvh * 2 + 0] = jnp.concatenate([kk, zrow], axis=0)
        kvar_ref[kvh * 2 + 1] = jnp.concatenate([zrow, kk], axis=0)
    vvar_ref[0] = jnp.where(left, v, 0.0).astype(BF16)
    vvar_ref[1] = jnp.where(left, 0.0, vswap).astype(BF16)
    vvar_ref[2] = jnp.where(left, vswap, 0.0).astype(BF16)
    vvar_ref[3] = jnp.where(left, 0.0, v).astype(BF16)

    q_ref[...] = _dot(h, wq_ref[...]).astype(BF16)
    row_half = lax.broadcasted_iota(jnp.int32, (2 * BLOCK, 1), 0) < BLOCK
    for j in range(nq):
        if j == 0:
            sel = jnp.where(i == 0, 1, 0)
        elif j == nq - 1:
            sel = jnp.where(i == nt - 1, 2, 0)
        else:
            sel = 0
        rows = pl.ds(j * BLOCK, BLOCK)
        keys = pl.ds(j * BLOCK, 3 * BLOCK)
        pairs = []
        for kvh in range(N_KV_HEADS):
            qs = jnp.concatenate(
                [q_ref[rows, pl.ds((2 * kvh) * LANES, LANES)],
                 q_ref[rows, pl.ds((2 * kvh + 1) * LANES, LANES)]], axis=0)
            acc = None
            for slot in range(2):
                c = kvh * 2 + slot
                s = _dot(qs, kvar_ref[c, :, keys]) + bias_ref[sel, c]
                snk = jnp.where(row_half, sink_ref[4 * kvh + slot], sink_ref[4 * kvh + 2 + slot])
                m = jnp.maximum(jnp.max(s, axis=-1, keepdims=True), snk)
                p = jnp.exp(s - m)
                den = jnp.sum(p, axis=-1, keepdims=True) + jnp.exp(snk - m)
                pv = _dot(p.astype(BF16), vvar_ref[c, keys, :]) * (1.0 / den)
                acc = pv if acc is None else acc + pv
            pairs += [acc[:BLOCK], acc[BLOCK:]]
        attn_ref[rows, :] = jnp.concatenate(pairs, axis=1).astype(BF16)

    qm = _dot(h, wqm_ref[...]).astype(BF16)
    for hd in range(MEM_HEADS):
        cols = pl.ds(hd * MEM_HEAD_DIM, MEM_HEAD_DIM)
        s = _dot(qm[:, hd * MEM_HEAD_DIM:(hd + 1) * MEM_HEAD_DIM], ktm_ref[0, cols, :])
        m = jnp.max(s, axis=-1, keepdims=True)
        p = jnp.exp(s - m)
        den = jnp.sum(p, axis=-1, keepdims=True)
        mo_ref[:, cols] = (_dot(p.astype(BF16), vm_ref[0, :, cols]) * (1.0 / den)).astype(BF16)

    z = jax.nn.gelu(_dot(h, wz_ref[...]))
    u_ref[...] = z[:, :SGU_WIDTH]
    vz = z[:, SGU_WIDTH:]
    mu = jnp.mean(vz, axis=-1, keepdims=True)
    vc = vz - mu
    var = jnp.mean(vc * vc, axis=-1, keepdims=True)
    vln_ref[...] = (vc * lax.rsqrt(var + EPS) * lng_ref[...] + lnb_ref[...]).astype(BF16)
    for n in range(ts // SGU_CHUNK):
        rows = pl.ds(n * SGU_CHUNK, SGU_CHUNK)
        for grp in range(SGU_GROUPS):
            cols = pl.ds(grp * LANES, LANES)
            sv = _dot(sguw_ref[grp], vln_ref[rows, cols]) + sgub_ref[:, cols]
            sgu_ref[rows, cols] = (u_ref[rows, cols] * sv).astype(BF16)

    merged = None
    for br, (src_ref, w_ref) in enumerate(((attn_ref, wba_ref), (sgu_ref, wbs_ref), (mo_ref, wbm_ref))):
        gate = jax.nn.sigmoid(_dot(h, wgl_ref[:, pl.ds(br * D_MODEL, D_MODEL)]))
        term = gate * _dot(src_ref[...], w_ref[...])
        merged = term if merged is None else merged + term
    o_ref[0] = x_ref[0] + _dot(merged.astype(BF16), wo_ref[...])


def _mixer(x, ktm, vm, bucket, rel_bias, sink, g_mix, wq, wkv, wz, wqm, wgl,
           sguw, sgub, lng, lnb, wba, wbs, wbm, wo):
    B, S, D = x.shape
    M = vm.shape[1]
    ts = TILE
    assert S % ts == 0 and ts % BLOCK == 0 and ts // BLOCK >= 2 and S // BLOCK >= 2
    nt = S // ts
    r = ts // BLOCK
    nblk = S // BLOCK
    w = ts + 2 * BLOCK
    smem = pl.BlockSpec(memory_space=pltpu.SMEM)
    in_specs = [
        pl.BlockSpec((1, ts, D), lambda b, i: (b, i, 0)),
        pl.BlockSpec((1, BLOCK, D), lambda b, i: (b, jnp.maximum(i * r - 1, 0), 0)),
        pl.BlockSpec((1, BLOCK, D), lambda b, i: (b, jnp.minimum((i + 1) * r, nblk - 1), 0)),
        pl.BlockSpec((1, MEM_WIDTH, M), lambda b, i: (b, 0, 0)),
        pl.BlockSpec((1, M, MEM_WIDTH), lambda b, i: (b, 0, 0)),
        _const_spec(bucket.shape), smem, smem,
    ] + [_const_spec(a.shape) for a in (g_mix, wq, wkv, wz, wqm, wgl, sguw, sgub, lng, lnb,
                                        wba, wbs, wbm, wo)]
    scratch = [
        pltpu.VMEM((3, 4, 2 * BLOCK, 3 * BLOCK), F32),
        pltpu.VMEM((w, D), BF16),
        pltpu.VMEM((4, KV_W, w), BF16),
        pltpu.VMEM((4, w, KV_W), BF16),
        pltpu.VMEM((ts, Q_W), BF16),
        pltpu.VMEM((ts, Q_W), BF16),
        pltpu.VMEM((ts, SGU_WIDTH), F32),
        pltpu.VMEM((ts, SGU_WIDTH), BF16),
        pltpu.VMEM((ts, SGU_WIDTH), BF16),
        pltpu.VMEM((ts, MEM_WIDTH), BF16),
    ]
    return pl.pallas_call(
        _mixer_kernel,
        grid=(B, nt),
        in_specs=in_specs,
        out_specs=pl.BlockSpec((1, ts, D), lambda b, i: (b, i, 0)),
        out_shape=jax.ShapeDtypeStruct((B, S, D), F32),
        scratch_shapes=scratch,
        compiler_params=pltpu.CompilerParams(
            dimension_semantics=("arbitrary", "arbitrary"), vmem_limit_bytes=VMEM_LIMIT),
        name="mixer",
    )(x, x, x, ktm, vm, bucket, rel_bias, sink, g_mix, wq, wkv, wz, wqm, wgl,
      sguw, sgub, lng, lnb, wba, wbs, wbm, wo)


def _ffn_kernel(x_ref, xp_ref, xn_ref, g_ref, wup_ref, cw_ref, cb_ref, wdn_ref, gf_ref,
                o_ref, hext_ref, ag_ref, au_ref, *, final_norm):
    ts = x_ref.shape[1]
    pad = BF16_ROWS
    i = pl.program_id(1)
    nt = pl.num_programs(1)
    g = g_ref[...]
    hp = _rms(xp_ref[0], g) * jnp.where(i > 0, 1.0, 0.0)
    hn = _rms(xn_ref[0], g) * jnp.where(i < nt - 1, 1.0, 0.0)
    hext_ref[0:pad] = hp.astype(BF16)
    hext_ref[pad:pad + ts] = _rms(x_ref[0], g).astype(BF16)
    hext_ref[pad + ts:] = hn.astype(BF16)

    def conv(a_ref, col0):
        cols = pl.ds(col0, FFN_COLS)
        return (a_ref[pl.ds(pad - 1, ts), :] * cw_ref[0:1, cols]
                + a_ref[pl.ds(pad, ts), :] * cw_ref[1:2, cols]
                + a_ref[pl.ds(pad + 1, ts), :] * cw_ref[2:3, cols]
                + cb_ref[:, cols])

    acc = x_ref[0]
    for c in range(D_FF // FFN_COLS):
        hx = hext_ref[...]
        ag_ref[...] = _dot(hx, wup_ref[:, pl.ds(c * FFN_COLS, FFN_COLS)])
        au_ref[...] = _dot(hx, wup_ref[:, pl.ds(D_FF + c * FFN_COLS, FFN_COLS)])
        act = jax.nn.gelu(conv(ag_ref, c * FFN_COLS)) * conv(au_ref, D_FF + c * FFN_COLS)
        acc = acc + _dot(act.astype(BF16), wdn_ref[pl.ds(c * FFN_COLS, FFN_COLS), :])
    o_ref[0] = _rms(acc, gf_ref[...]) if final_norm else acc


def _ffn(x, g_ffn, wup, conv_w, conv_b, wdn, g_final, final_norm):
    B, S, D = x.shape
    ts = TILE
    pad = BF16_ROWS
    assert S % ts == 0 and ts % pad == 0 and D_FF % FFN_COLS == 0
    nt = S // ts
    r = ts // pad
    nblk = S // pad
    in_specs = [
        pl.BlockSpec((1, ts, D), lambda b, i: (b, i, 0)),
        pl.BlockSpec((1, pad, D), lambda b, i: (b, jnp.maximum(i * r - 1, 0), 0)),
        pl.BlockSpec((1, pad, D), lambda b, i: (b, jnp.minimum((i + 1) * r, nblk - 1), 0)),
    ] + [_const_spec(a.shape) for a in (g_ffn, wup, conv_w, conv_b, wdn, g_final)]
    return pl.pallas_call(
        functools.partial(_ffn_kernel, final_norm=final_norm),
        grid=(B, nt),
        in_specs=in_specs,
        out_specs=pl.BlockSpec((1, ts, D), lambda b, i: (b, i, 0)),
        out_shape=jax.ShapeDtypeStruct((B, S, D), F32),
        scratch_shapes=[
            pltpu.VMEM((ts + 2 * pad, D), BF16),
            pltpu.VMEM((ts + 2 * pad, FFN_COLS), F32),
            pltpu.VMEM((ts + 2 * pad, FFN_COLS), F32),
        ],
        compiler_params=pltpu.CompilerParams(
            dimension_semantics=("arbitrary", "arbitrary"), vmem_limit_bytes=VMEM_LIMIT),
        name="ffn",
    )(x, x, x, g_ffn, wup, conv_w, conv_b, wdn, g_final)


def _t5_bucket(rel):
    nb = N_BUCKETS // 2
    max_exact = nb // 2
    ret = (rel > 0).astype(jnp.int32) * nb
    n = jnp.abs(rel)
    nf = jnp.maximum(n, 1).astype(jnp.float32)
    large = max_exact + (jnp.log(nf / max_exact) / math.log(MAX_DISTANCE / max_exact)
                         * (nb - max_exact)).astype(jnp.int32)
    large = jnp.minimum(large, nb - 1)
    return ret + jnp.where(n < max_exact, n, large)


def kernel(x_prompt, x_sample, mem_prompt, mem_sample, rel_bias, g_mix, w_in, attn_sink,
           sgu_w, sgu_b, sgu_ln_g, sgu_ln_b, g_mem, w_mem_kv, w_br_attn, w_br_sgu, w_br_mem,
           w_out, g_ffn, w_up, conv_w, conv_b, w_down, g_final):
    depth = w_in.shape[0]
    rel = (jnp.arange(3 * BLOCK)[None, :] - BLOCK) - jnp.arange(BLOCK)[:, None]
    bucket = _t5_bucket(rel)
    row = lambda a: a.reshape(1, -1)

    def encoder(x, mem):
        for l in range(depth):
            wi = w_in[l]
            wq = (wi[:, :OFF_K] * (HEAD_DIM ** -0.5)).astype(BF16)
            wkv = wi[:, OFF_K:OFF_Z].astype(BF16)
            wz = wi[:, OFF_Z:OFF_QM].astype(BF16)
            wqm = wi[:, OFF_QM:OFF_GL].astype(BF16)
            wgl = wi[:, OFF_GL:].astype(BF16)
            sgub = jnp.repeat(sgu_b[l].T, SGU_WIDTH // SGU_GROUPS, axis=1)
            ktm, vm = _mem_kv(mem, row(g_mem[l]), w_mem_kv[l].astype(BF16))
            x = _mixer(x, ktm, vm, bucket, rel_bias, attn_sink[l], row(g_mix[l]),
                       wq, wkv, wz, wqm, wgl, sgu_w[l].astype(BF16), sgub,
                       row(sgu_ln_g[l]), row(sgu_ln_b[l]),
                       w_br_attn[l].astype(BF16), w_br_sgu[l].astype(BF16),
                       w_br_mem[l].astype(BF16), w_out[l].astype(BF16))
            x = _ffn(x, row(g_ffn[l]), w_up[l].astype(BF16), conv_w[l], row(conv_b[l]),
                     w_down[l].astype(BF16), row(g_final), final_norm=(l == depth - 1))
        return x

    return encoder(x_prompt, mem_prompt), encoder(x_sample, mem_sample)
```

```python
import functools
import math

import jax
import jax.numpy as jnp
from jax import lax
from jax.experimental import pallas as pl
from jax.experimental.pallas import tpu as pltpu

D_MODEL = 1024
HEAD_DIM = 64
N_HEADS = 8
N_KV_HEADS = 2
WINDOW = 128
BLOCK = 128
N_BUCKETS = 32
MAX_DISTANCE = 128
SGU_WIDTH = 512
SGU_GROUPS = 4
SGU_CHUNK = 128
MEM_HEADS = 4
MEM_HEAD_DIM = 128
MEM_WIDTH = MEM_HEADS * MEM_HEAD_DIM
D_FF = 2816
EPS = 1e-6
NEG = -1e30

Q_W = N_HEADS * HEAD_DIM
KV_W = N_KV_HEADS * HEAD_DIM
OFF_K = Q_W
OFF_Z = Q_W + 2 * KV_W
OFF_QM = OFF_Z + 2 * SGU_WIDTH
OFF_GL = OFF_QM + MEM_WIDTH

LANES = 128
BF16_ROWS = 16
VMEM_LIMIT = 56 * 1024 * 1024

TILE = 512
FFN_COLS = 256
DOWN_PIECE = 4

BF16 = jnp.bfloat16
F32 = jnp.float32


def _rms(x, g):
    return x * lax.rsqrt(jnp.mean(x * x, axis=-1, keepdims=True) + EPS) * g


_GELU_C0 = math.sqrt(2.0 / math.pi)
_GELU_C1 = _GELU_C0 * 0.044715


def _gelu_cdf(x):
    return 0.5 + 0.5 * jnp.tanh(x * (_GELU_C0 + _GELU_C1 * (x * x)))


def _dot(a, b):
    return jnp.dot(a, b, preferred_element_type=F32)


def _const_spec(shape):
    nd = len(shape)
    return pl.BlockSpec(shape, lambda *_: (0,) * nd, pipeline_mode=pl.Buffered(1))


def _mem_kv_kernel(mem_ref, g_ref, w_ref, kt_ref, v_ref):
    h = _rms(mem_ref[0], g_ref[...]).astype(BF16)
    kv = _dot(h, w_ref[...])
    k = kv[:, :MEM_WIDTH] * (MEM_HEAD_DIM ** -0.5)
    kt_ref[0] = k.T.astype(BF16)
    v_ref[0] = kv[:, MEM_WIDTH:].astype(BF16)


def _mem_kv(mem, g_mem, w_mem_kv):
    B, M, D = mem.shape
    return pl.pallas_call(
        _mem_kv_kernel,
        grid=(B,),
        in_specs=[
            pl.BlockSpec((1, M, D), lambda b: (b, 0, 0)),
            _const_spec((1, D)),
            _const_spec((D, 2 * MEM_WIDTH)),
        ],
        out_specs=[
            pl.BlockSpec((1, MEM_WIDTH, M), lambda b: (b, 0, 0)),
            pl.BlockSpec((1, M, MEM_WIDTH), lambda b: (b, 0, 0)),
        ],
        out_shape=[
            jax.ShapeDtypeStruct((B, MEM_WIDTH, M), BF16),
            jax.ShapeDtypeStruct((B, M, MEM_WIDTH), BF16),
        ],
        compiler_params=pltpu.CompilerParams(
            dimension_semantics=("arbitrary",), vmem_limit_bytes=VMEM_LIMIT),
        name="mem_kv",
    )(mem, g_mem, w_mem_kv)


def _mixer_kernel(x_ref, xp_ref, xn_ref, ktm_ref, vm_ref, bucket_ref, relb_ref, sink_ref,
                  g_ref, wq_ref, wkv_ref, wz_ref, wqm_ref, wgl_ref,
                  sguw_ref, sgub_ref, lng_ref, lnb_ref,
                  wba_ref, wbs_ref, wbm_ref, wo_ref,
                  o_ref,
                  bias_ref, hext_ref, kvar_ref, vvar_ref, q_ref, attn_ref,
                  u_ref, vln_ref, sgu_ref, mo_ref):
    ts = x_ref.shape[1]
    nq = ts // BLOCK
    i = pl.program_id(1)
    nt = pl.num_programs(1)
    first_step = jnp.logical_and(pl.program_id(0) == 0, i == 0)

    @pl.when(first_step)
    def _():
        bk = bucket_ref[...]
        qi = lax.broadcasted_iota(jnp.int32, (BLOCK, 3 * BLOCK), 0)
        kj = lax.broadcasted_iota(jnp.int32, (BLOCK, 3 * BLOCK), 1)
        in_window = jnp.abs(kj - BLOCK - qi) <= WINDOW
        for hd in range(N_HEADS):
            acc = jnp.zeros((BLOCK, 3 * BLOCK), F32)
            for b in range(N_BUCKETS):
                acc = jnp.where(bk == b, relb_ref[b, hd], acc)
            base = jnp.where(in_window, acc, NEG)
            kv, rem = divmod(hd, N_HEADS // N_KV_HEADS)
            half, slot = divmod(rem, 2)
            c = kv * 2 + slot
            rows = pl.ds(half * BLOCK, BLOCK)
            bias_ref[0, c, rows, :] = base
            bias_ref[1, c, rows, :] = jnp.where(kj < BLOCK, NEG, base)
            bias_ref[2, c, rows, :] = jnp.where(kj >= 2 * BLOCK, NEG, base)

    g = g_ref[...]
    hext_ref[0:BLOCK] = _rms(xp_ref[0], g).astype(BF16)
    hext_ref[BLOCK:BLOCK + ts] = _rms(x_ref[0], g).astype(BF16)
    hext_ref[BLOCK + ts:] = _rms(xn_ref[0], g).astype(BF16)
    h = hext_ref[BLOCK:BLOCK + ts]

    w = ts + 2 * BLOCK
    kv_all = _dot(hext_ref[...], wkv_ref[...])
    kt = kv_all[:, :KV_W].T.astype(BF16)
    v = kv_all[:, KV_W:]
    vswap = pltpu.roll(v, HEAD_DIM, axis=1)
    left = lax.broadcasted_iota(jnp.int32, (w, KV_W), 1) < HEAD_DIM
    zrow = jnp.zeros((HEAD_DIM, w), BF16)
    for kvh in range(N_KV_HEADS):
        kk = kt[kvh * HEAD_DIM:(kvh + 1) * HEAD_DIM]
        kvar_ref[kvh * 2 + 0] = jnp.concatenate([kk, zrow], axis=0)
        kvar_ref[kvh * 2 + 1] = jnp.concatenate([zrow, kk], axis=0)
    vvar_ref[0] = jnp.where(left, v, 0.0).astype(BF16)
    vvar_ref[1] = jnp.where(left, 0.0, vswap).astype(BF16)
    vvar_ref[2] = jnp.where(left, vswap, 0.0).astype(BF16)
    vvar_ref[3] = jnp.where(left, 0.0, v).astype(BF16)

    q_ref[...] = _dot(h, wq_ref[...]).astype(BF16)
    row_half = lax.broadcasted_iota(jnp.int32, (2 * BLOCK, 1), 0) < BLOCK
    for j in range(nq):
        if j == 0:
            sel = jnp.where(i == 0, 1, 0)
        elif j == nq - 1:
            sel = jnp.where(i == nt - 1, 2, 0)
        else:
            sel = 0
        rows = pl.ds(j * BLOCK, BLOCK)
        keys = pl.ds(j * BLOCK, 3 * BLOCK)
        pairs = []
        for kvh in range(N_KV_HEADS):
            qs = jnp.concatenate(
                [q_ref[rows, pl.ds((2 * kvh) * LANES, LANES)],
                 q_ref[rows, pl.ds((2 * kvh + 1) * LANES, LANES)]], axis=0)
            acc = None
            for slot in range(2):
                c = kvh * 2 + slot
                s = _dot(qs, kvar_ref[c, :, keys]) + bias_ref[sel, c]
                snk = jnp.where(row_half, sink_ref[4 * kvh + slot], sink_ref[4 * kvh + 2 + slot])
                m = jnp.maximum(jnp.max(s, axis=-1, keepdims=True), snk)
                p = jnp.exp(s - m)
                den = jnp.sum(p, axis=-1, keepdims=True) + jnp.exp(snk - m)
                pv = _dot(p.astype(BF16), vvar_ref[c, keys, :]) * (1.0 / den)
                acc = pv if acc is None else acc + pv
            pairs += [acc[:BLOCK], acc[BLOCK:]]
        attn_ref[rows, :] = jnp.concatenate(pairs, axis=1).astype(BF16)

    qm = _dot(h, wqm_ref[...]).astype(BF16)
    for hd in range(MEM_HEADS):
        cols = pl.ds(hd * MEM_HEAD_DIM, MEM_HEAD_DIM)
        s = _dot(qm[:, hd * MEM_HEAD_DIM:(hd + 1) * MEM_HEAD_DIM], ktm_ref[0, cols, :])
        m = jnp.max(s, axis=-1, keepdims=True)
        p = jnp.exp(s - m)
        den = jnp.sum(p, axis=-1, keepdims=True)
        mo_ref[:, cols] = (_dot(p.astype(BF16), vm_ref[0, :, cols]) * (1.0 / den)).astype(BF16)

    z = _dot(h, wz_ref[...])
    z = z * _gelu_cdf(z)
    u_ref[...] = z[:, :SGU_WIDTH]
    vz = z[:, SGU_WIDTH:]
    mu = jnp.mean(vz, axis=-1, keepdims=True)
    vc = vz - mu
    var = jnp.mean(vc * vc, axis=-1, keepdims=True)
    vln_ref[...] = (vc * lax.rsqrt(var + EPS) * lng_ref[...] + lnb_ref[...]).astype(BF16)
    for n in range(ts // SGU_CHUNK):
        rows = pl.ds(n * SGU_CHUNK, SGU_CHUNK)
        for grp in range(SGU_GROUPS):
            cols = pl.ds(grp * LANES, LANES)
            sv = _dot(sguw_ref[grp], vln_ref[rows, cols]) + sgub_ref[:, cols]
            sgu_ref[rows, cols] = (u_ref[rows, cols] * sv).astype(BF16)

    merged = None
    for br, (src_ref, w_ref) in enumerate(((attn_ref, wba_ref), (sgu_ref, wbs_ref), (mo_ref, wbm_ref))):
        gate = jax.nn.sigmoid(_dot(h, wgl_ref[:, pl.ds(br * D_MODEL, D_MODEL)]))
        term = gate * _dot(src_ref[...], w_ref[...])
        merged = term if merged is None else merged + term
    o_ref[0] = x_ref[0] + _dot(merged.astype(BF16), wo_ref[...])


def _mixer(x, ktm, vm, bucket, rel_bias, sink, g_mix, wq, wkv, wz, wqm, wgl,
           sguw, sgub, lng, lnb, wba, wbs, wbm, wo):
    B, S, D = x.shape
    M = vm.shape[1]
    ts = TILE
    assert S % ts == 0 and ts % BLOCK == 0 and ts // BLOCK >= 2 and S // BLOCK >= 2
    nt = S // ts
    r = ts // BLOCK
    nblk = S // BLOCK
    w = ts + 2 * BLOCK
    smem = pl.BlockSpec(memory_space=pltpu.SMEM)
    in_specs = [
        pl.BlockSpec((1, ts, D), lambda b, i: (b, i, 0)),
        pl.BlockSpec((1, BLOCK, D), lambda b, i: (b, jnp.maximum(i * r - 1, 0), 0)),
        pl.BlockSpec((1, BLOCK, D), lambda b, i: (b, jnp.minimum((i + 1) * r, nblk - 1), 0)),
        pl.BlockSpec((1, MEM_WIDTH, M), lambda b, i: (b, 0, 0)),
        pl.BlockSpec((1, M, MEM_WIDTH), lambda b, i: (b, 0, 0)),
        _const_spec(bucket.shape), smem, smem,
    ] + [_const_spec(a.shape) for a in (g_mix, wq, wkv, wz, wqm, wgl, sguw, sgub, lng, lnb,
                                        wba, wbs, wbm, wo)]
    scratch = [
        pltpu.VMEM((3, 4, 2 * BLOCK, 3 * BLOCK), F32),
        pltpu.VMEM((w, D), BF16),
        pltpu.VMEM((4, KV_W, w), BF16),
        pltpu.VMEM((4, w, KV_W), BF16),
        pltpu.VMEM((ts, Q_W), BF16),
        pltpu.VMEM((ts, Q_W), BF16),
        pltpu.VMEM((ts, SGU_WIDTH), F32),
        pltpu.VMEM((ts, SGU_WIDTH), BF16),
        pltpu.VMEM((ts, SGU_WIDTH), BF16),
        pltpu.VMEM((ts, MEM_WIDTH), BF16),
    ]
    return pl.pallas_call(
        _mixer_kernel,
        grid=(B, nt),
        in_specs=in_specs,
        out_specs=pl.BlockSpec((1, ts, D), lambda b, i: (b, i, 0)),
        out_shape=jax.ShapeDtypeStruct((B, S, D), F32),
        scratch_shapes=scratch,
        compiler_params=pltpu.CompilerParams(
            dimension_semantics=("arbitrary", "arbitrary"), vmem_limit_bytes=VMEM_LIMIT),
        name="mixer",
    )(x, x, x, ktm, vm, bucket, rel_bias, sink, g_mix, wq, wkv, wz, wqm, wgl,
      sguw, sgub, lng, lnb, wba, wbs, wbm, wo)


def _ffn_kernel(x_ref, xp_ref, xn_ref, g_ref, wup_ref, cw_ref, cb_ref, wdn_ref, gf_ref,
                o_ref, hext_ref, act_ref, *, final_norm):
    ts = x_ref.shape[1]
    pad = BF16_ROWS
    i = pl.program_id(1)
    nt = pl.num_programs(1)
    g = g_ref[...]
    hp = _rms(xp_ref[0], g) * jnp.where(i > 0, 1.0, 0.0)
    hn = _rms(xn_ref[0], g) * jnp.where(i < nt - 1, 1.0, 0.0)
    hext_ref[0:pad] = hp.astype(BF16)
    hext_ref[pad:pad + ts] = _rms(x_ref[0], g).astype(BF16)
    hext_ref[pad + ts:] = hn.astype(BF16)

    rows_ext = ts + 2 * pad

    def up(col0):
        return _dot(hext_ref[...], wup_ref[:, pl.ds(col0, FFN_COLS)])

    def conv(a, col0):
        cols = pl.ds(col0, FFN_COLS)
        prev = pltpu.roll(a, 1, axis=0)
        nxt = pltpu.roll(a, rows_ext - 1, axis=0)
        mid = slice(pad, pad + ts)
        return (prev[mid] * cw_ref[0:1, cols] + a[mid] * cw_ref[1:2, cols]
                + nxt[mid] * cw_ref[2:3, cols] + cb_ref[:, cols])

    n_chunks = D_FF // FFN_COLS
    y = x_ref[0]
    done = 0
    for c in range(n_chunks):
        a_gate = up(c * FFN_COLS)
        a_up = up(D_FF + c * FFN_COLS)
        if c > 0 and (c % DOWN_PIECE == 0):
            k0, k1 = done * FFN_COLS, c * FFN_COLS
            y = y + _dot(act_ref[:, k0:k1], wdn_ref[k0:k1, :])
            done = c
        gate = conv(a_gate, c * FFN_COLS)
        act = (gate * conv(a_up, D_FF + c * FFN_COLS)) * _gelu_cdf(gate)
        act_ref[:, pl.ds(c * FFN_COLS, FFN_COLS)] = act.astype(BF16)
    y = y + _dot(act_ref[:, done * FFN_COLS:], wdn_ref[done * FFN_COLS:, :])
    o_ref[0] = _rms(y, gf_ref[...]) if final_norm else y


def _ffn(x, g_ffn, wup, conv_w, conv_b, wdn, g_final, final_norm):
    B, S, D = x.shape
    ts = TILE
    pad = BF16_ROWS
    assert S % ts == 0 and ts % pad == 0 and D_FF % FFN_COLS == 0
    nt = S // ts
    r = ts // pad
    nblk = S // pad
    in_specs = [
        pl.BlockSpec((1, ts, D), lambda b, i: (b, i, 0)),
        pl.BlockSpec((1, pad, D), lambda b, i: (b, jnp.maximum(i * r - 1, 0), 0)),
        pl.BlockSpec((1, pad, D), lambda b, i: (b, jnp.minimum((i + 1) * r, nblk - 1), 0)),
    ] + [_const_spec(a.shape) for a in (g_ffn, wup, conv_w, conv_b, wdn, g_final)]
    return pl.pallas_call(
        functools.partial(_ffn_kernel, final_norm=final_norm),
        grid=(B, nt),
        in_specs=in_specs,
        out_specs=pl.BlockSpec((1, ts, D), lambda b, i: (b, i, 0)),
        out_shape=jax.ShapeDtypeStruct((B, S, D), F32),
        scratch_shapes=[
            pltpu.VMEM((ts + 2 * pad, D), BF16),
            pltpu.VMEM((ts, D_FF), BF16),
        ],
        compiler_params=pltpu.CompilerParams(
            dimension_semantics=("arbitrary", "arbitrary"), vmem_limit_bytes=VMEM_LIMIT),
        name="ffn",
    )(x, x, x, g_ffn, wup, conv_w, conv_b, wdn, g_final)


def _t5_bucket(rel):
    nb = N_BUCKETS // 2
    max_exact = nb // 2
    ret = (rel > 0).astype(jnp.int32) * nb
    n = jnp.abs(rel)
    nf = jnp.maximum(n, 1).astype(jnp.float32)
    large = max_exact + (jnp.log(nf / max_exact) / math.log(MAX_DISTANCE / max_exact)
                         * (nb - max_exact)).astype(jnp.int32)
    large = jnp.minimum(large, nb - 1)
    return ret + jnp.where(n < max_exact, n, large)


def kernel(x_prompt, x_sample, mem_prompt, mem_sample, rel_bias, g_mix, w_in, attn_sink,
           sgu_w, sgu_b, sgu_ln_g, sgu_ln_b, g_mem, w_mem_kv, w_br_attn, w_br_sgu, w_br_mem,
           w_out, g_ffn, w_up, conv_w, conv_b, w_down, g_final):
    depth = w_in.shape[0]
    rel = (jnp.arange(3 * BLOCK)[None, :] - BLOCK) - jnp.arange(BLOCK)[:, None]
    bucket = _t5_bucket(rel)
    row = lambda a: a.reshape(1, -1)

    def encoder(x, mem):
        for l in range(depth):
            wi = w_in[l]
            wq = (wi[:, :OFF_K] * (HEAD_DIM ** -0.5)).astype(BF16)
            wkv = wi[:, OFF_K:OFF_Z].astype(BF16)
            wz = wi[:, OFF_Z:OFF_QM].astype(BF16)
            wqm = wi[:, OFF_QM:OFF_GL].astype(BF16)
            wgl = wi[:, OFF_GL:].astype(BF16)
            sgub = jnp.repeat(sgu_b[l].T, SGU_WIDTH // SGU_GROUPS, axis=1)
            ktm, vm = _mem_kv(mem, row(g_mem[l]), w_mem_kv[l].astype(BF16))
            x = _mixer(x, ktm, vm, bucket, rel_bias, attn_sink[l], row(g_mix[l]),
                       wq, wkv, wz, wqm, wgl, sgu_w[l].astype(BF16), sgub,
                       row(sgu_ln_g[l]), row(sgu_ln_b[l]),
                       w_br_attn[l].astype(BF16), w_br_sgu[l].astype(BF16),
                       w_br_mem[l].astype(BF16), w_out[l].astype(BF16))
            x = _ffn(x, row(g_ffn[l]), w_up[l].astype(BF16), conv_w[l], row(conv_b[l]),
                     w_down[l].astype(BF16), row(g_final), final_norm=(l == depth - 1))
        return x

    return encoder(x_prompt, mem_prompt), encoder(x_sample, mem_sample)
```

```python
import functools
import math

import jax
import jax.numpy as jnp
from jax import lax
from jax.experimental import pallas as pl
from jax.experimental.pallas import tpu as pltpu

D_MODEL = 1024
HEAD_DIM = 64
N_HEADS = 8
N_KV_HEADS = 2
WINDOW = 128
BLOCK = 128
N_BUCKETS = 32
MAX_DISTANCE = 128
SGU_WIDTH = 512
SGU_GROUPS = 4
SGU_CHUNK = 128
MEM_HEADS = 4
MEM_HEAD_DIM = 128
MEM_WIDTH = MEM_HEADS * MEM_HEAD_DIM
D_FF = 2816
EPS = 1e-6
NEG = -1e30

Q_W = N_HEADS * HEAD_DIM
KV_W = N_KV_HEADS * HEAD_DIM
OFF_K = Q_W
OFF_Z = Q_W + 2 * KV_W
OFF_QM = OFF_Z + 2 * SGU_WIDTH
OFF_GL = OFF_QM + MEM_WIDTH

LANES = 128
BF16_ROWS = 16
VMEM_LIMIT = 56 * 1024 * 1024

TILE = 512
FFN_COLS = 256
DOWN_PIECE = 4
PROJ_COLS = 256

BF16 = jnp.bfloat16
F32 = jnp.float32


def _rms(x, g):
    return x * lax.rsqrt(jnp.mean(x * x, axis=-1, keepdims=True) + EPS) * g


_GELU_C0 = math.sqrt(2.0 / math.pi)
_GELU_C1 = _GELU_C0 * 0.044715


def _gelu_cdf(x):
    return 0.5 + 0.5 * jnp.tanh(x * (_GELU_C0 + _GELU_C1 * (x * x)))


def _dot(a, b):
    return jnp.dot(a, b, preferred_element_type=F32)


def _const_spec(shape):
    nd = len(shape)
    return pl.BlockSpec(shape, lambda *_: (0,) * nd, pipeline_mode=pl.Buffered(1))


def _mem_kv_kernel(mem_ref, g_ref, w_ref, kt_ref, v_ref):
    h = _rms(mem_ref[0], g_ref[...]).astype(BF16)
    kv = _dot(h, w_ref[...])
    k = kv[:, :MEM_WIDTH] * (MEM_HEAD_DIM ** -0.5)
    kt_ref[0] = k.T.astype(BF16)
    v_ref[0] = kv[:, MEM_WIDTH:].astype(BF16)


def _mem_kv(mem, g_mem, w_mem_kv):
    B, M, D = mem.shape
    return pl.pallas_call(
        _mem_kv_kernel,
        grid=(B,),
        in_specs=[
            pl.BlockSpec((1, M, D), lambda b: (b, 0, 0)),
            _const_spec((1, D)),
            _const_spec((D, 2 * MEM_WIDTH)),
        ],
        out_specs=[
            pl.BlockSpec((1, MEM_WIDTH, M), lambda b: (b, 0, 0)),
            pl.BlockSpec((1, M, MEM_WIDTH), lambda b: (b, 0, 0)),
        ],
        out_shape=[
            jax.ShapeDtypeStruct((B, MEM_WIDTH, M), BF16),
            jax.ShapeDtypeStruct((B, M, MEM_WIDTH), BF16),
        ],
        compiler_params=pltpu.CompilerParams(
            dimension_semantics=("arbitrary",), vmem_limit_bytes=VMEM_LIMIT),
        name="mem_kv",
    )(mem, g_mem, w_mem_kv)


def _mixer_kernel(x_ref, xp_ref, xn_ref, ktm_ref, vm_ref, bucket_ref, relb_ref, sink_ref,
                  g_ref, wq_ref, wkv_ref, wz_ref, wqm_ref, wgl_ref,
                  sguw_ref, sgub_ref, lng_ref, lnb_ref,
                  wba_ref, wbs_ref, wbm_ref, wo_ref,
                  o_ref,
                  bias_ref, hext_ref, kvar_ref, vvar_ref, q_ref, attn_ref,
                  u_ref, vg_ref, vln_ref, sgu_ref, qm_ref, mo_ref, gate_ref):
    ts = x_ref.shape[1]
    nq = ts // BLOCK
    i = pl.program_id(1)
    nt = pl.num_programs(1)
    first_step = jnp.logical_and(pl.program_id(0) == 0, i == 0)

    @pl.when(first_step)
    def _():
        bk = bucket_ref[...]
        qi = lax.broadcasted_iota(jnp.int32, (BLOCK, 3 * BLOCK), 0)
        kj = lax.broadcasted_iota(jnp.int32, (BLOCK, 3 * BLOCK), 1)
        in_window = jnp.abs(kj - BLOCK - qi) <= WINDOW
        for hd in range(N_HEADS):
            acc = jnp.zeros((BLOCK, 3 * BLOCK), F32)
            for b in range(N_BUCKETS):
                acc = jnp.where(bk == b, relb_ref[b, hd], acc)
            base = jnp.where(in_window, acc, NEG)
            kv, rem = divmod(hd, N_HEADS // N_KV_HEADS)
            half, slot = divmod(rem, 2)
            c = kv * 2 + slot
            rows = pl.ds(half * BLOCK, BLOCK)
            bias_ref[0, c, rows, :] = base
            bias_ref[1, c, rows, :] = jnp.where(kj < BLOCK, NEG, base)
            bias_ref[2, c, rows, :] = jnp.where(kj >= 2 * BLOCK, NEG, base)

    g = g_ref[...]
    hext_ref[0:BLOCK] = _rms(xp_ref[0], g).astype(BF16)
    hext_ref[BLOCK:BLOCK + ts] = _rms(x_ref[0], g).astype(BF16)
    hext_ref[BLOCK + ts:] = _rms(xn_ref[0], g).astype(BF16)
    h = hext_ref[BLOCK:BLOCK + ts]

    w = ts + 2 * BLOCK
    kv_all = _dot(hext_ref[...], wkv_ref[...])
    kt = kv_all[:, :KV_W].T.astype(BF16)
    v = kv_all[:, KV_W:]
    vswap = pltpu.roll(v, HEAD_DIM, axis=1)
    left = lax.broadcasted_iota(jnp.int32, (w, KV_W), 1) < HEAD_DIM
    zrow = jnp.zeros((HEAD_DIM, w), BF16)
    for kvh in range(N_KV_HEADS):
        kk = kt[kvh * HEAD_DIM:(kvh + 1) * HEAD_DIM]
        kvar_ref[kvh * 2 + 0] = jnp.concatenate([kk, zrow], axis=0)
        kvar_ref[kvh * 2 + 1] = jnp.concatenate([zrow, kk], axis=0)
    vvar_ref[0] = jnp.where(left, v, 0.0).astype(BF16)
    vvar_ref[1] = jnp.where(left, 0.0, vswap).astype(BF16)
    vvar_ref[2] = jnp.where(left, vswap, 0.0).astype(BF16)
    vvar_ref[3] = jnp.where(left, 0.0, v).astype(BF16)

    def proj_chunk(w_ref, col0):
        return _dot(h, w_ref[:, pl.ds(col0, PROJ_COLS)])

    def qm_task(col0):
        qm_ref[:, pl.ds(col0, PROJ_COLS)] = proj_chunk(wqm_ref, col0).astype(BF16)

    def z_task(col0):
        z = proj_chunk(wz_ref, col0)
        z = z * _gelu_cdf(z)
        if col0 < SGU_WIDTH:
            u_ref[:, pl.ds(col0, PROJ_COLS)] = z
        else:
            vg_ref[:, pl.ds(col0 - SGU_WIDTH, PROJ_COLS)] = z

    def gate_task(col0):
        gate_ref[:, pl.ds(col0, PROJ_COLS)] = jax.nn.sigmoid(proj_chunk(wgl_ref, col0))

    tasks = ([functools.partial(qm_task, c0) for c0 in range(0, MEM_WIDTH, PROJ_COLS)]
             + [functools.partial(z_task, c0) for c0 in range(0, 2 * SGU_WIDTH, PROJ_COLS)]
             + [functools.partial(gate_task, c0) for c0 in range(0, 3 * D_MODEL, PROJ_COLS)])

    def run_task():
        if tasks:
            tasks.pop(0)()

    q_ref[...] = _dot(h, wq_ref[...]).astype(BF16)
    row_half = lax.broadcasted_iota(jnp.int32, (2 * BLOCK, 1), 0) < BLOCK
    for j in range(nq):
        if j == 0:
            sel = jnp.where(i == 0, 1, 0)
        elif j == nq - 1:
            sel = jnp.where(i == nt - 1, 2, 0)
        else:
            sel = 0
        rows = pl.ds(j * BLOCK, BLOCK)
        keys = pl.ds(j * BLOCK, 3 * BLOCK)
        pairs = []
        for kvh in range(N_KV_HEADS):
            qs = jnp.concatenate(
                [q_ref[rows, pl.ds((2 * kvh) * LANES, LANES)],
                 q_ref[rows, pl.ds((2 * kvh + 1) * LANES, LANES)]], axis=0)
            acc = None
            for slot in range(2):
                c = kvh * 2 + slot
                s = _dot(qs, kvar_ref[c, :, keys]) + bias_ref[sel, c]
                snk = jnp.where(row_half, sink_ref[4 * kvh + slot], sink_ref[4 * kvh + 2 + slot])
                m = jnp.maximum(jnp.max(s, axis=-1, keepdims=True), snk)
                p = jnp.exp(s - m)
                den = jnp.sum(p, axis=-1, keepdims=True) + jnp.exp(snk - m)
                pv = _dot(p.astype(BF16), vvar_ref[c, keys, :]) * (1.0 / den)
                acc = pv if acc is None else acc + pv
                run_task()
            pairs += [acc[:BLOCK], acc[BLOCK:]]
        attn_ref[rows, :] = jnp.concatenate(pairs, axis=1).astype(BF16)

    for hd in range(MEM_HEADS):
        cols = pl.ds(hd * MEM_HEAD_DIM, MEM_HEAD_DIM)
        s = _dot(qm_ref[:, cols], ktm_ref[0, cols, :])
        m = jnp.max(s, axis=-1, keepdims=True)
        p = jnp.exp(s - m)
        den = jnp.sum(p, axis=-1, keepdims=True)
        mo_ref[:, cols] = (_dot(p.astype(BF16), vm_ref[0, :, cols]) * (1.0 / den)).astype(BF16)
        run_task()
    while tasks:
        run_task()

    vz = vg_ref[...]
    mu = jnp.mean(vz, axis=-1, keepdims=True)
    vc = vz - mu
    var = jnp.mean(vc * vc, axis=-1, keepdims=True)
    vln_ref[...] = (vc * lax.rsqrt(var + EPS) * lng_ref[...] + lnb_ref[...]).astype(BF16)
    for n in range(ts // SGU_CHUNK):
        rows = pl.ds(n * SGU_CHUNK, SGU_CHUNK)
        for grp in range(SGU_GROUPS):
            cols = pl.ds(grp * LANES, LANES)
            sv = _dot(sguw_ref[grp], vln_ref[rows, cols]) + sgub_ref[:, cols]
            sgu_ref[rows, cols] = (u_ref[rows, cols] * sv).astype(BF16)

    merged = None
    for br, (src_ref, w_ref) in enumerate(((attn_ref, wba_ref), (sgu_ref, wbs_ref), (mo_ref, wbm_ref))):
        term = gate_ref[:, pl.ds(br * D_MODEL, D_MODEL)] * _dot(src_ref[...], w_ref[...])
        merged = term if merged is None else merged + term
    o_ref[0] = x_ref[0] + _dot(merged.astype(BF16), wo_ref[...])


def _mixer(x, ktm, vm, bucket, rel_bias, sink, g_mix, wq, wkv, wz, wqm, wgl,
           sguw, sgub, lng, lnb, wba, wbs, wbm, wo):
    B, S, D = x.shape
    M = vm.shape[1]
    ts = TILE
    assert S % ts == 0 and ts % BLOCK == 0 and ts // BLOCK >= 2 and S // BLOCK >= 2
    nt = S // ts
    r = ts // BLOCK
    nblk = S // BLOCK
    w = ts + 2 * BLOCK
    smem = pl.BlockSpec(memory_space=pltpu.SMEM)
    in_specs = [
        pl.BlockSpec((1, ts, D), lambda b, i: (b, i, 0)),
        pl.BlockSpec((1, BLOCK, D), lambda b, i: (b, jnp.maximum(i * r - 1, 0), 0)),
        pl.BlockSpec((1, BLOCK, D), lambda b, i: (b, jnp.minimum((i + 1) * r, nblk - 1), 0)),
        pl.BlockSpec((1, MEM_WIDTH, M), lambda b, i: (b, 0, 0)),
        pl.BlockSpec((1, M, MEM_WIDTH), lambda b, i: (b, 0, 0)),
        _const_spec(bucket.shape), smem, smem,
    ] + [_const_spec(a.shape) for a in (g_mix, wq, wkv, wz, wqm, wgl, sguw, sgub, lng, lnb,
                                        wba, wbs, wbm, wo)]
    scratch = [
        pltpu.VMEM((3, 4, 2 * BLOCK, 3 * BLOCK), F32),
        pltpu.VMEM((w, D), BF16),
        pltpu.VMEM((4, KV_W, w), BF16),
        pltpu.VMEM((4, w, KV_W), BF16),
        pltpu.VMEM((ts, Q_W), BF16),
        pltpu.VMEM((ts, Q_W), BF16),
        pltpu.VMEM((ts, SGU_WIDTH), F32),
        pltpu.VMEM((ts, SGU_WIDTH), F32),
        pltpu.VMEM((ts, SGU_WIDTH), BF16),
        pltpu.VMEM((ts, SGU_WIDTH), BF16),
        pltpu.VMEM((ts, MEM_WIDTH), BF16),
        pltpu.VMEM((ts, MEM_WIDTH), BF16),
        pltpu.VMEM((ts, 3 * D_MODEL), F32),
    ]
    return pl.pallas_call(
        _mixer_kernel,
        grid=(B, nt),
        in_specs=in_specs,
        out_specs=pl.BlockSpec((1, ts, D), lambda b, i: (b, i, 0)),
        out_shape=jax.ShapeDtypeStruct((B, S, D), F32),
        scratch_shapes=scratch,
        compiler_params=pltpu.CompilerParams(
            dimension_semantics=("arbitrary", "arbitrary"), vmem_limit_bytes=VMEM_LIMIT),
        name="mixer",
    )(x, x, x, ktm, vm, bucket, rel_bias, sink, g_mix, wq, wkv, wz, wqm, wgl,
      sguw, sgub, lng, lnb, wba, wbs, wbm, wo)


def _ffn_kernel(x_ref, xp_ref, xn_ref, g_ref, wup_ref, cw_ref, cb_ref, wdn_ref, gf_ref,
                o_ref, hext_ref, act_ref, *, final_norm):
    ts = x_ref.shape[1]
    pad = BF16_ROWS
    i = pl.program_id(1)
    nt = pl.num_programs(1)
    g = g_ref[...]
    hp = _rms(xp_ref[0], g) * jnp.where(i > 0, 1.0, 0.0)
    hn = _rms(xn_ref[0], g) * jnp.where(i < nt - 1, 1.0, 0.0)
    hext_ref[0:pad] = hp.astype(BF16)
    hext_ref[pad:pad + ts] = _rms(x_ref[0], g).astype(BF16)
    hext_ref[pad + ts:] = hn.astype(BF16)

    rows_ext = ts + 2 * pad

    def up(col0):
        return _dot(hext_ref[...], wup_ref[:, pl.ds(col0, FFN_COLS)])

    def conv(a, col0):
        cols = pl.ds(col0, FFN_COLS)
        prev = pltpu.roll(a, 1, axis=0)
        nxt = pltpu.roll(a, rows_ext - 1, axis=0)
        mid = slice(pad, pad + ts)
        return (prev[mid] * cw_ref[0:1, cols] + a[mid] * cw_ref[1:2, cols]
                + nxt[mid] * cw_ref[2:3, cols] + cb_ref[:, cols])

    n_chunks = D_FF // FFN_COLS
    y = x_ref[0]
    done = 0
    for c in range(n_chunks):
        a_gate = up(c * FFN_COLS)
        a_up = up(D_FF + c * FFN_COLS)
        if c > 0 and (c % DOWN_PIECE == 0):
            k0, k1 = done * FFN_COLS, c * FFN_COLS
            y = y + _dot(act_ref[:, k0:k1], wdn_ref[k0:k1, :])
            done = c
        gate = conv(a_gate, c * FFN_COLS)
        act = (gate * conv(a_up, D_FF + c * FFN_COLS)) * _gelu_cdf(gate)
        act_ref[:, pl.ds(c * FFN_COLS, FFN_COLS)] = act.astype(BF16)
    y = y + _dot(act_ref[:, done * FFN_COLS:], wdn_ref[done * FFN_COLS:, :])
    o_ref[0] = _rms(y, gf_ref[...]) if final_norm else y


def _ffn(x, g_ffn, wup, conv_w, conv_b, wdn, g_final, final_norm):
    B, S, D = x.shape
    ts = TILE
    pad = BF16_ROWS
    assert S % ts == 0 and ts % pad == 0 and D_FF % FFN_COLS == 0
    nt = S // ts
    r = ts // pad
    nblk = S // pad
    in_specs = [
        pl.BlockSpec((1, ts, D), lambda b, i: (b, i, 0)),
        pl.BlockSpec((1, pad, D), lambda b, i: (b, jnp.maximum(i * r - 1, 0), 0)),
        pl.BlockSpec((1, pad, D), lambda b, i: (b, jnp.minimum((i + 1) * r, nblk - 1), 0)),
    ] + [_const_spec(a.shape) for a in (g_ffn, wup, conv_w, conv_b, wdn, g_final)]
    return pl.pallas_call(
        functools.partial(_ffn_kernel, final_norm=final_norm),
        grid=(B, nt),
        in_specs=in_specs,
        out_specs=pl.BlockSpec((1, ts, D), lambda b, i: (b, i, 0)),
        out_shape=jax.ShapeDtypeStruct((B, S, D), F32),
        scratch_shapes=[
            pltpu.VMEM((ts + 2 * pad, D), BF16),
            pltpu.VMEM((ts, D_FF), BF16),
        ],
        compiler_params=pltpu.CompilerParams(
            dimension_semantics=("arbitrary", "arbitrary"), vmem_limit_bytes=VMEM_LIMIT),
        name="ffn",
    )(x, x, x, g_ffn, wup, conv_w, conv_b, wdn, g_final)


def _t5_bucket(rel):
    nb = N_BUCKETS // 2
    max_exact = nb // 2
    ret = (rel > 0).astype(jnp.int32) * nb
    n = jnp.abs(rel)
    nf = jnp.maximum(n, 1).astype(jnp.float32)
    large = max_exact + (jnp.log(nf / max_exact) / math.log(MAX_DISTANCE / max_exact)
                         * (nb - max_exact)).astype(jnp.int32)
    large = jnp.minimum(large, nb - 1)
    return ret + jnp.where(n < max_exact, n, large)


def kernel(x_prompt, x_sample, mem_prompt, mem_sample, rel_bias, g_mix, w_in, attn_sink,
           sgu_w, sgu_b, sgu_ln_g, sgu_ln_b, g_mem, w_mem_kv, w_br_attn, w_br_sgu, w_br_mem,
           w_out, g_ffn, w_up, conv_w, conv_b, w_down, g_final):
    depth = w_in.shape[0]
    rel = (jnp.arange(3 * BLOCK)[None, :] - BLOCK) - jnp.arange(BLOCK)[:, None]
    bucket = _t5_bucket(rel)
    row = lambda a: a.reshape(1, -1)

    def encoder(x, mem):
        for l in range(depth):
            wi = w_in[l]
            wq = (wi[:, :OFF_K] * (HEAD_DIM ** -0.5)).astype(BF16)
            wkv = wi[:, OFF_K:OFF_Z].astype(BF16)
            wz = wi[:, OFF_Z:OFF_QM].astype(BF16)
            wqm = wi[:, OFF_QM:OFF_GL].astype(BF16)
            wgl = wi[:, OFF_GL:].astype(BF16)
            sgub = jnp.repeat(sgu_b[l].T, SGU_WIDTH // SGU_GROUPS, axis=1)
            ktm, vm = _mem_kv(mem, row(g_mem[l]), w_mem_kv[l].astype(BF16))
            x = _mixer(x, ktm, vm, bucket, rel_bias, attn_sink[l], row(g_mix[l]),
                       wq, wkv, wz, wqm, wgl, sgu_w[l].astype(BF16), sgub,
                       row(sgu_ln_g[l]), row(sgu_ln_b[l]),
                       w_br_attn[l].astype(BF16), w_br_sgu[l].astype(BF16),
                       w_br_mem[l].astype(BF16), w_out[l].astype(BF16))
            x = _ffn(x, row(g_ffn[l]), w_up[l].astype(BF16), conv_w[l], row(conv_b[l]),
                     w_down[l].astype(BF16), row(g_final), final_norm=(l == depth - 1))
        return x

    return encoder(x_prompt, mem_prompt), encoder(x_sample, mem_sample)
```

```python
import functools
import math

import jax
import jax.numpy as jnp
from jax import lax
from jax.experimental import pallas as pl
from jax.experimental.pallas import tpu as pltpu

D_MODEL = 1024
HEAD_DIM = 64
N_HEADS = 8
N_KV_HEADS = 2
WINDOW = 128
BLOCK = 128
N_BUCKETS = 32
MAX_DISTANCE = 128
SGU_WIDTH = 512
SGU_GROUPS = 4
SGU_CHUNK = 128
MEM_HEADS = 4
MEM_HEAD_DIM = 128
MEM_WIDTH = MEM_HEADS * MEM_HEAD_DIM
D_FF = 2816
EPS = 1e-6
NEG = -1e30

Q_W = N_HEADS * HEAD_DIM
KV_W = N_KV_HEADS * HEAD_DIM
OFF_K = Q_W
OFF_Z = Q_W + 2 * KV_W
OFF_QM = OFF_Z + 2 * SGU_WIDTH
OFF_GL = OFF_QM + MEM_WIDTH

LANES = 128
SUBLANES = 8
BF16_ROWS = 16
VMEM_LIMIT = 56 * 1024 * 1024

TILE = 512
FFN_TILE = 1024
FFN_COLS = 256
DOWN_FLUSH = (4, 8, 10)
PROJ_COLS = 512

BF16 = jnp.bfloat16
F32 = jnp.float32


def _rms(x, g):
    return x * lax.rsqrt(jnp.mean(x * x, axis=-1, keepdims=True) + EPS) * g


_GELU_C0 = math.sqrt(2.0 / math.pi)
_GELU_C1 = _GELU_C0 * 0.044715


def _gelu_cdf(x):
    return 0.5 + 0.5 * jnp.tanh(x * (_GELU_C0 + _GELU_C1 * (x * x)))


def _dot(a, b):
    return jnp.dot(a, b, preferred_element_type=F32)


def _const_spec(shape):
    nd = len(shape)
    return pl.BlockSpec(shape, lambda *_: (0,) * nd, pipeline_mode=pl.Buffered(1))


def _mem_kv_kernel(mem_ref, g_ref, w_ref, kt_ref, v_ref):
    h = _rms(mem_ref[0], g_ref[...]).astype(BF16)
    kv = _dot(h, w_ref[...])
    k = kv[:, :MEM_WIDTH] * (MEM_HEAD_DIM ** -0.5)
    kt_ref[0] = k.T.astype(BF16)
    v_ref[0] = kv[:, MEM_WIDTH:].astype(BF16)


def _mem_kv(mem, g_mem, w_mem_kv):
    B, M, D = mem.shape
    return pl.pallas_call(
        _mem_kv_kernel,
        grid=(B,),
        in_specs=[
            pl.BlockSpec((1, M, D), lambda b: (b, 0, 0)),
            _const_spec((1, D)),
            _const_spec((D, 2 * MEM_WIDTH)),
        ],
        out_specs=[
            pl.BlockSpec((1, MEM_WIDTH, M), lambda b: (b, 0, 0)),
            pl.BlockSpec((1, M, MEM_WIDTH), lambda b: (b, 0, 0)),
        ],
        out_shape=[
            jax.ShapeDtypeStruct((B, MEM_WIDTH, M), BF16),
            jax.ShapeDtypeStruct((B, M, MEM_WIDTH), BF16),
        ],
        compiler_params=pltpu.CompilerParams(
            dimension_semantics=("arbitrary",), vmem_limit_bytes=VMEM_LIMIT),
        name="mem_kv",
    )(mem, g_mem, w_mem_kv)


def _mixer_kernel(x_ref, xp_ref, xn_ref, ktm_ref, vm_ref, bucket_ref, relb_ref, sink_ref,
                  g_ref, wq_ref, wkv_ref, wz_ref, wqm_ref, wgl_ref,
                  sguw_ref, sgub_ref, lng_ref, lnb_ref,
                  wba_ref, wbs_ref, wbm_ref, wo_ref,
                  o_ref,
                  bias_ref, hext_ref, kvar_ref, vvar_ref, q_ref, attn_ref,
                  u_ref, vln_ref, sgu_ref, qm_ref, mo_ref, gate_ref, merged_ref, mrg_ref):
    ts = x_ref.shape[1]
    nq = ts // BLOCK
    i = pl.program_id(1)
    nt = pl.num_programs(1)
    first_step = jnp.logical_and(pl.program_id(0) == 0, i == 0)

    @pl.when(first_step)
    def _():
        bk = bucket_ref[...]
        qi = lax.broadcasted_iota(jnp.int32, (BLOCK, 3 * BLOCK), 0)
        kj = lax.broadcasted_iota(jnp.int32, (BLOCK, 3 * BLOCK), 1)
        in_window = jnp.abs(kj - BLOCK - qi) <= WINDOW
        for hd in range(N_HEADS):
            acc = jnp.zeros((BLOCK, 3 * BLOCK), F32)
            for b in range(N_BUCKETS):
                acc = jnp.where(bk == b, relb_ref[b, hd], acc)
            base = jnp.where(in_window, acc, NEG)
            kv, rem = divmod(hd, N_HEADS // N_KV_HEADS)
            half, slot = divmod(rem, 2)
            rows = pl.ds(half * BLOCK, BLOCK)
            cols = pl.ds(slot * 3 * BLOCK, 3 * BLOCK)
            bias_ref[0, kv, rows, cols] = base
            bias_ref[1, kv, rows, cols] = jnp.where(kj < BLOCK, NEG, base)
            bias_ref[2, kv, rows, cols] = jnp.where(kj >= 2 * BLOCK, NEG, base)

    g = g_ref[...]
    hext_ref[0:BLOCK] = _rms(xp_ref[0], g).astype(BF16)
    hext_ref[BLOCK:BLOCK + ts] = _rms(x_ref[0], g).astype(BF16)
    hext_ref[BLOCK + ts:] = _rms(xn_ref[0], g).astype(BF16)
    h = hext_ref[BLOCK:BLOCK + ts]

    w = ts + 2 * BLOCK
    kv_all = _dot(hext_ref[...], wkv_ref[...])
    kt = kv_all[:, :KV_W].T.astype(BF16)
    v = kv_all[:, KV_W:]
    vswap = pltpu.roll(v, HEAD_DIM, axis=1)
    left = lax.broadcasted_iota(jnp.int32, (w, KV_W), 1) < HEAD_DIM
    zrow = jnp.zeros((HEAD_DIM, w), BF16)
    for kvh in range(N_KV_HEADS):
        kk = kt[kvh * HEAD_DIM:(kvh + 1) * HEAD_DIM]
        kvar_ref[kvh * 2 + 0] = jnp.concatenate([kk, zrow], axis=0)
        kvar_ref[kvh * 2 + 1] = jnp.concatenate([zrow, kk], axis=0)
    vvar_ref[0] = jnp.where(left, v, 0.0).astype(BF16)
    vvar_ref[1] = jnp.where(left, 0.0, vswap).astype(BF16)
    vvar_ref[2] = jnp.where(left, vswap, 0.0).astype(BF16)
    vvar_ref[3] = jnp.where(left, 0.0, v).astype(BF16)

    def proj_chunk(w_ref, col0):
        return _dot(h, w_ref[:, pl.ds(col0, PROJ_COLS)])

    def qm_task(col0):
        qm_ref[:, pl.ds(col0, PROJ_COLS)] = proj_chunk(wqm_ref, col0).astype(BF16)

    def mem_head_task(hd):
        cols = pl.ds(hd * MEM_HEAD_DIM, MEM_HEAD_DIM)
        s = _dot(qm_ref[:, cols], ktm_ref[0, cols, :])
        m = jnp.max(s, axis=-1, keepdims=True)
        p = jnp.exp(s - m)
        den = jnp.sum(p, axis=-1, keepdims=True)
        mo_ref[:, cols] = (_dot(p.astype(BF16), vm_ref[0, :, cols]) * (1.0 / den)).astype(BF16)

    def sgu_u_task():
        z = _dot(h, wz_ref[:, pl.ds(0, SGU_WIDTH)])
        u_ref[...] = z * _gelu_cdf(z)

    def sgu_v_task():
        z = _dot(h, wz_ref[:, pl.ds(SGU_WIDTH, SGU_WIDTH)])
        vz = z * _gelu_cdf(z)
        mu = jnp.mean(vz, axis=-1, keepdims=True)
        vc = vz - mu
        var = jnp.mean(vc * vc, axis=-1, keepdims=True)
        vln_ref[...] = (vc * lax.rsqrt(var + EPS) * lng_ref[...] + lnb_ref[...]).astype(BF16)

    def sgu_task(n):
        rows = pl.ds(n * SGU_CHUNK, SGU_CHUNK)
        for grp in range(SGU_GROUPS):
            cols = pl.ds(grp * LANES, LANES)
            sv = _dot(sguw_ref[grp], vln_ref[rows, cols]) + sgub_ref[:, cols]
            sgu_ref[rows, cols] = (u_ref[rows, cols] * sv).astype(BF16)

    def gate_task(br, col0):
        cols = pl.ds(br * D_MODEL + col0, PROJ_COLS)
        gate_ref[:, cols] = jax.nn.sigmoid(proj_chunk(wgl_ref, br * D_MODEL + col0))

    def branch_term(br, src_ref, w_ref, col0):
        gate = gate_ref[:, pl.ds(br * D_MODEL + col0, PROJ_COLS)]
        return gate * _dot(src_ref[...], w_ref[:, pl.ds(col0, PROJ_COLS)])

    def branch_task(br, src_ref, w_ref, first, col0):
        cols = pl.ds(col0, PROJ_COLS)
        term = branch_term(br, src_ref, w_ref, col0)
        merged_ref[:, cols] = term if first else merged_ref[:, cols] + term

    part = functools.partial
    col_chunks = range(0, D_MODEL, PROJ_COLS)
    tasks = ([part(qm_task, c0) for c0 in range(0, MEM_WIDTH, PROJ_COLS)]
             + [part(mem_head_task, hd) for hd in range(MEM_HEADS)]
             + [part(gate_task, 2, c0) for c0 in col_chunks]
             + [part(branch_task, 2, mo_ref, wbm_ref, True, c0) for c0 in col_chunks]
             + [sgu_u_task, sgu_v_task]
             + [part(sgu_task, n) for n in range(ts // SGU_CHUNK)]
             + [part(gate_task, 1, c0) for c0 in col_chunks]
             + [part(branch_task, 1, sgu_ref, wbs_ref, False, c0) for c0 in col_chunks]
             + [part(gate_task, 0, c0) for c0 in col_chunks])
    n_units = nq * N_KV_HEADS
    tasks_per_unit = -(-len(tasks) // n_units)

    def run_tasks(n):
        for _ in range(min(n, len(tasks))):
            tasks.pop(0)()

    q_ref[...] = _dot(h, wq_ref[...]).astype(BF16)
    row_half = lax.broadcasted_iota(jnp.int32, (2 * BLOCK, 1), 0) < BLOCK
    lane_left = lax.broadcasted_iota(jnp.int32, (2 * BLOCK, LANES), 1) < HEAD_DIM
    for j in range(nq):
        if j == 0:
            sel = jnp.where(i == 0, 1, 0)
        elif j == nq - 1:
            sel = jnp.where(i == nt - 1, 2, 0)
        else:
            sel = 0
        rows = pl.ds(j * BLOCK, BLOCK)
        keys = pl.ds(j * BLOCK, 3 * BLOCK)
        pairs = []
        for kvh in range(N_KV_HEADS):
            qs = jnp.concatenate(
                [q_ref[rows, pl.ds((2 * kvh) * LANES, LANES)],
                 q_ref[rows, pl.ds((2 * kvh + 1) * LANES, LANES)]], axis=0)
            kcat = jnp.concatenate([kvar_ref[2 * kvh, :, keys], kvar_ref[2 * kvh + 1, :, keys]], axis=1)
            s = jnp.dot(qs, kcat, preferred_element_type=F32) + bias_ref[sel, kvh]
            ps, invs = [], []
            for slot in range(2):
                sh = s[:, slot * 3 * BLOCK:(slot + 1) * 3 * BLOCK]
                snk = jnp.where(row_half, sink_ref[4 * kvh + slot], sink_ref[4 * kvh + 2 + slot])
                m = jnp.maximum(jnp.max(sh, axis=-1, keepdims=True), snk)
                p = jnp.exp(sh - m)
                den = jnp.sum(p, axis=-1, keepdims=True) + jnp.exp(snk - m)
                ps.append(p.astype(BF16))
                invs.append(1.0 / den)
            vcat = jnp.concatenate([vvar_ref[2 * kvh, keys, :], vvar_ref[2 * kvh + 1, keys, :]], axis=0)
            pv = jnp.dot(jnp.concatenate(ps, axis=1), vcat, preferred_element_type=F32)
            pv = pv * jnp.where(lane_left, invs[0], invs[1])
            pairs += [pv[:BLOCK], pv[BLOCK:]]
            run_tasks(tasks_per_unit)
        attn_ref[rows, :] = jnp.concatenate(pairs, axis=1).astype(BF16)
    run_tasks(len(tasks))

    for c0 in col_chunks:
        cols = pl.ds(c0, PROJ_COLS)
        mrg_ref[:, cols] = (merged_ref[:, cols] + branch_term(0, attn_ref, wba_ref, c0)).astype(BF16)
    o_ref[0] = x_ref[0] + _dot(mrg_ref[...], wo_ref[...])


def _mixer(x, ktm, vm, bucket, rel_bias, sink, g_mix, wq, wkv, wz, wqm, wgl,
           sguw, sgub, lng, lnb, wba, wbs, wbm, wo):
    B, S, D = x.shape
    M = vm.shape[1]
    ts = TILE
    assert S % ts == 0 and ts % BLOCK == 0 and ts // BLOCK >= 2 and S // BLOCK >= 2
    nt = S // ts
    r = ts // BLOCK
    nblk = S // BLOCK
    w = ts + 2 * BLOCK
    smem = pl.BlockSpec(memory_space=pltpu.SMEM)
    in_specs = [
        pl.BlockSpec((1, ts, D), lambda b, i: (b, i, 0)),
        pl.BlockSpec((1, BLOCK, D), lambda b, i: (b, jnp.maximum(i * r - 1, 0), 0)),
        pl.BlockSpec((1, BLOCK, D), lambda b, i: (b, jnp.minimum((i + 1) * r, nblk - 1), 0)),
        pl.BlockSpec((1, MEM_WIDTH, M), lambda b, i: (b, 0, 0)),
        pl.BlockSpec((1, M, MEM_WIDTH), lambda b, i: (b, 0, 0)),
        _const_spec(bucket.shape), smem, smem,
    ] + [_const_spec(a.shape) for a in (g_mix, wq, wkv, wz, wqm, wgl, sguw, sgub, lng, lnb,
                                        wba, wbs, wbm, wo)]
    scratch = [
        pltpu.VMEM((3, N_KV_HEADS, 2 * BLOCK, 6 * BLOCK), F32),
        pltpu.VMEM((w, D), BF16),
        pltpu.VMEM((4, KV_W, w), BF16),
        pltpu.VMEM((4, w, KV_W), BF16),
        pltpu.VMEM((ts, Q_W), BF16),
        pltpu.VMEM((ts, Q_W), BF16),
        pltpu.VMEM((ts, SGU_WIDTH), F32),
        pltpu.VMEM((ts, SGU_WIDTH), BF16),
        pltpu.VMEM((ts, SGU_WIDTH), BF16),
        pltpu.VMEM((ts, MEM_WIDTH), BF16),
        pltpu.VMEM((ts, MEM_WIDTH), BF16),
        pltpu.VMEM((ts, 3 * D_MODEL), F32),
        pltpu.VMEM((ts, D_MODEL), F32),
        pltpu.VMEM((ts, D_MODEL), BF16),
    ]
    return pl.pallas_call(
        _mixer_kernel,
        grid=(B, nt),
        in_specs=in_specs,
        out_specs=pl.BlockSpec((1, ts, D), lambda b, i: (b, i, 0)),
        out_shape=jax.ShapeDtypeStruct((B, S, D), F32),
        scratch_shapes=scratch,
        compiler_params=pltpu.CompilerParams(
            dimension_semantics=("arbitrary", "arbitrary"), vmem_limit_bytes=VMEM_LIMIT),
        name="mixer",
    )(x, x, x, ktm, vm, bucket, rel_bias, sink, g_mix, wq, wkv, wz, wqm, wgl,
      sguw, sgub, lng, lnb, wba, wbs, wbm, wo)


def _ffn_kernel(x_ref, xp_ref, xn_ref, g_ref, wup_ref, cw_ref, cb_ref, wdn_ref, gf_ref,
                o_ref, hext_ref, act_ref, *, final_norm):
    ts = x_ref.shape[1]
    pad = BF16_ROWS
    i = pl.program_id(1)
    nt = pl.num_programs(1)
    g = g_ref[...]
    hp = _rms(xp_ref[0], g) * jnp.where(i > 0, 1.0, 0.0)
    hn = _rms(xn_ref[0], g) * jnp.where(i < nt - 1, 1.0, 0.0)
    hext_ref[0:pad] = hp.astype(BF16)
    hext_ref[pad:pad + ts] = _rms(x_ref[0], g).astype(BF16)
    hext_ref[pad + ts:] = hn.astype(BF16)

    rows_ext = ts + 2 * pad

    def up(col0):
        return _dot(hext_ref[...], wup_ref[:, pl.ds(col0, FFN_COLS)])

    def conv(a, col0):
        cols = pl.ds(col0, FFN_COLS)
        prev = pltpu.roll(a, 1, axis=0)
        nxt = pltpu.roll(a, rows_ext - 1, axis=0)
        mid = slice(pad, pad + ts)
        return (prev[mid] * cw_ref[0:1, cols] + a[mid] * cw_ref[1:2, cols]
                + nxt[mid] * cw_ref[2:3, cols] + cb_ref[:, cols])

    n_chunks = D_FF // FFN_COLS
    y = x_ref[0]
    done = 0
    for c in range(n_chunks):
        a_gate = up(c * FFN_COLS)
        a_up = up(D_FF + c * FFN_COLS)
        if c in DOWN_FLUSH:
            k0, k1 = done * FFN_COLS, c * FFN_COLS
            y = y + _dot(act_ref[:, k0:k1], wdn_ref[k0:k1, :])
            done = c
        gate = conv(a_gate, c * FFN_COLS)
        act = (gate * conv(a_up, D_FF + c * FFN_COLS)) * _gelu_cdf(gate)
        act_ref[:, pl.ds(c * FFN_COLS, FFN_COLS)] = act.astype(BF16)
    y = y + _dot(act_ref[:, done * FFN_COLS:], wdn_ref[done * FFN_COLS:, :])
    o_ref[0] = _rms(y, gf_ref[...]) if final_norm else y


def _ffn(x, g_ffn, wup, conv_w, conv_b, wdn, g_final, final_norm):
    B, S, D = x.shape
    ts = FFN_TILE
    pad = BF16_ROWS
    assert S % ts == 0 and ts % pad == 0 and D_FF % FFN_COLS == 0
    nt = S // ts
    r = ts // pad
    nblk = S // pad
    in_specs = [
        pl.BlockSpec((1, ts, D), lambda b, i: (b, i, 0)),
        pl.BlockSpec((1, pad, D), lambda b, i: (b, jnp.maximum(i * r - 1, 0), 0)),
        pl.BlockSpec((1, pad, D), lambda b, i: (b, jnp.minimum((i + 1) * r, nblk - 1), 0)),
    ] + [_const_spec(a.shape) for a in (g_ffn, wup, conv_w, conv_b, wdn, g_final)]
    return pl.pallas_call(
        functools.partial(_ffn_kernel, final_norm=final_norm),
        grid=(B, nt),
        in_specs=in_specs,
        out_specs=pl.BlockSpec((1, ts, D), lambda b, i: (b, i, 0)),
        out_shape=jax.ShapeDtypeStruct((B, S, D), F32),
        scratch_shapes=[
            pltpu.VMEM((ts + 2 * pad, D), BF16),
            pltpu.VMEM((ts, D_FF), BF16),
        ],
        compiler_params=pltpu.CompilerParams(
            dimension_semantics=("arbitrary", "arbitrary"), vmem_limit_bytes=VMEM_LIMIT),
        name="ffn",
    )(x, x, x, g_ffn, wup, conv_w, conv_b, wdn, g_final)


def _t5_bucket(rel):
    nb = N_BUCKETS // 2
    max_exact = nb // 2
    ret = (rel > 0).astype(jnp.int32) * nb
    n = jnp.abs(rel)
    nf = jnp.maximum(n, 1).astype(jnp.float32)
    large = max_exact + (jnp.log(nf / max_exact) / math.log(MAX_DISTANCE / max_exact)
                         * (nb - max_exact)).astype(jnp.int32)
    large = jnp.minimum(large, nb - 1)
    return ret + jnp.where(n < max_exact, n, large)


def kernel(x_prompt, x_sample, mem_prompt, mem_sample, rel_bias, g_mix, w_in, attn_sink,
           sgu_w, sgu_b, sgu_ln_g, sgu_ln_b, g_mem, w_mem_kv, w_br_attn, w_br_sgu, w_br_mem,
           w_out, g_ffn, w_up, conv_w, conv_b, w_down, g_final):
    depth = w_in.shape[0]
    rel = (jnp.arange(3 * BLOCK)[None, :] - BLOCK) - jnp.arange(BLOCK)[:, None]
    bucket = _t5_bucket(rel)
    row = lambda a: a.reshape(1, -1)

    def encoder(x, mem):
        for l in range(depth):
            wi = w_in[l]
            wq = (wi[:, :OFF_K] * (HEAD_DIM ** -0.5)).astype(BF16)
            wkv = wi[:, OFF_K:OFF_Z].astype(BF16)
            wz = wi[:, OFF_Z:OFF_QM].astype(BF16)
            wqm = wi[:, OFF_QM:OFF_GL].astype(BF16)
            wgl = wi[:, OFF_GL:].astype(BF16)
            sgub = jnp.repeat(sgu_b[l].T, SGU_WIDTH // SGU_GROUPS, axis=1)
            ktm, vm = _mem_kv(mem, row(g_mem[l]), w_mem_kv[l].astype(BF16))
            x = _mixer(x, ktm, vm, bucket, rel_bias, attn_sink[l], row(g_mix[l]),
                       wq, wkv, wz, wqm, wgl, sgu_w[l].astype(BF16), sgub,
                       row(sgu_ln_g[l]), row(sgu_ln_b[l]),
                       w_br_attn[l].astype(BF16), w_br_sgu[l].astype(BF16),
                       w_br_mem[l].astype(BF16), w_out[l].astype(BF16))
            x = _ffn(x, row(g_ffn[l]), w_up[l].astype(BF16), conv_w[l], row(conv_b[l]),
                     w_down[l].astype(BF16), row(g_final), final_norm=(l == depth - 1))
        return x

    return encoder(x_prompt, mem_prompt), encoder(x_sample, mem_sample)
```

```python
import functools
import math

import jax
import jax.numpy as jnp
from jax import lax
from jax.experimental import pallas as pl
from jax.experimental.pallas import tpu as pltpu

D_MODEL = 1024
HEAD_DIM = 64
N_HEADS = 8
N_KV_HEADS = 2
WINDOW = 128
BLOCK = 128
N_BUCKETS = 32
MAX_DISTANCE = 128
SGU_WIDTH = 512
SGU_GROUPS = 4
SGU_CHUNK = 128
MEM_HEADS = 4
MEM_HEAD_DIM = 128
MEM_WIDTH = MEM_HEADS * MEM_HEAD_DIM
D_FF = 2816
EPS = 1e-6
NEG = -1e30

Q_W = N_HEADS * HEAD_DIM
KV_W = N_KV_HEADS * HEAD_DIM
OFF_K = Q_W
OFF_Z = Q_W + 2 * KV_W
OFF_QM = OFF_Z + 2 * SGU_WIDTH
OFF_GL = OFF_QM + MEM_WIDTH

LANES = 128
SUBLANES = 8
BF16_ROWS = 16
VMEM_LIMIT = 56 * 1024 * 1024

TILE = 512
FFN_TILE = 512
FFN_COLS = 256
DOWN_FLUSH = (4, 8)
PROJ_COLS = 512

BF16 = jnp.bfloat16
F32 = jnp.float32


def _rms(x, g):
    return x * lax.rsqrt(jnp.mean(x * x, axis=-1, keepdims=True) + EPS) * g


_GELU_C0 = math.sqrt(2.0 / math.pi)
_GELU_C1 = _GELU_C0 * 0.044715


def _gelu_cdf(x):
    return 0.5 + 0.5 * jnp.tanh(x * (_GELU_C0 + _GELU_C1 * (x * x)))


def _dot(a, b):
    return jnp.dot(a, b, preferred_element_type=F32)


def _const_spec(shape):
    nd = len(shape)
    return pl.BlockSpec(shape, lambda *_: (0,) * nd, pipeline_mode=pl.Buffered(1))


def _mem_kv_kernel(mem_ref, g_ref, w_ref, kt_ref, v_ref):
    h = _rms(mem_ref[0], g_ref[...]).astype(BF16)
    kv = _dot(h, w_ref[...])
    k = kv[:, :MEM_WIDTH] * (MEM_HEAD_DIM ** -0.5)
    kt_ref[0] = k.T.astype(BF16)
    v_ref[0] = kv[:, MEM_WIDTH:].astype(BF16)


def _mem_kv(mem, g_mem, w_mem_kv):
    B, M, D = mem.shape
    return pl.pallas_call(
        _mem_kv_kernel,
        grid=(B,),
        in_specs=[
            pl.BlockSpec((1, M, D), lambda b: (b, 0, 0)),
            _const_spec((1, D)),
            _const_spec((D, 2 * MEM_WIDTH)),
        ],
        out_specs=[
            pl.BlockSpec((1, MEM_WIDTH, M), lambda b: (b, 0, 0)),
            pl.BlockSpec((1, M, MEM_WIDTH), lambda b: (b, 0, 0)),
        ],
        out_shape=[
            jax.ShapeDtypeStruct((B, MEM_WIDTH, M), BF16),
            jax.ShapeDtypeStruct((B, M, MEM_WIDTH), BF16),
        ],
        compiler_params=pltpu.CompilerParams(
            dimension_semantics=("arbitrary",), vmem_limit_bytes=VMEM_LIMIT),
        name="mem_kv",
    )(mem, g_mem, w_mem_kv)


def _mixer_kernel(x_ref, xp_ref, xn_ref, ktm_ref, vm_ref, bucket_ref, relb_ref, sink_ref,
                  g_ref, wq_ref, wkv_ref, wz_ref, wqm_ref, wgl_ref,
                  sguw_ref, sgub_ref, lng_ref, lnb_ref,
                  wba_ref, wbs_ref, wbm_ref, wo_ref,
                  o_ref,
                  bias_ref, hext_ref, kvar_ref, vvar_ref, q_ref, attn_ref,
                  u_ref, vln_ref, sgu_ref, qm_ref, mo_ref, gate_ref, merged_ref, mrg_ref):
    ts = x_ref.shape[1]
    nq = ts // BLOCK
    i = pl.program_id(1)
    nt = pl.num_programs(1)
    first_step = jnp.logical_and(pl.program_id(0) == 0, i == 0)

    @pl.when(first_step)
    def _():
        bk = bucket_ref[...]
        qi = lax.broadcasted_iota(jnp.int32, (BLOCK, 3 * BLOCK), 0)
        kj = lax.broadcasted_iota(jnp.int32, (BLOCK, 3 * BLOCK), 1)
        in_window = jnp.abs(kj - BLOCK - qi) <= WINDOW
        for hd in range(N_HEADS):
            acc = jnp.zeros((BLOCK, 3 * BLOCK), F32)
            for b in range(N_BUCKETS):
                acc = jnp.where(bk == b, relb_ref[b, hd], acc)
            base = jnp.where(in_window, acc, NEG)
            kv, rem = divmod(hd, N_HEADS // N_KV_HEADS)
            half, slot = divmod(rem, 2)
            rows = pl.ds(half * BLOCK, BLOCK)
            cols = pl.ds(slot * 3 * BLOCK, 3 * BLOCK)
            bias_ref[0, kv, rows, cols] = base
            bias_ref[1, kv, rows, cols] = jnp.where(kj < BLOCK, NEG, base)
            bias_ref[2, kv, rows, cols] = jnp.where(kj >= 2 * BLOCK, NEG, base)

    g = g_ref[...]
    hext_ref[0:BLOCK] = _rms(xp_ref[0], g).astype(BF16)
    hext_ref[BLOCK:BLOCK + ts] = _rms(x_ref[0], g).astype(BF16)
    hext_ref[BLOCK + ts:] = _rms(xn_ref[0], g).astype(BF16)
    h = hext_ref[BLOCK:BLOCK + ts]

    w = ts + 2 * BLOCK
    kv_all = _dot(hext_ref[...], wkv_ref[...])
    kt = kv_all[:, :KV_W].T.astype(BF16)
    v = kv_all[:, KV_W:]
    vswap = pltpu.roll(v, HEAD_DIM, axis=1)
    left = lax.broadcasted_iota(jnp.int32, (w, KV_W), 1) < HEAD_DIM
    zrow = jnp.zeros((HEAD_DIM, w), BF16)
    for kvh in range(N_KV_HEADS):
        kk = kt[kvh * HEAD_DIM:(kvh + 1) * HEAD_DIM]
        kvar_ref[kvh * 2 + 0] = jnp.concatenate([kk, zrow], axis=0)
        kvar_ref[kvh * 2 + 1] = jnp.concatenate([zrow, kk], axis=0)
    vvar_ref[0] = jnp.where(left, v, 0.0).astype(BF16)
    vvar_ref[1] = jnp.where(left, 0.0, vswap).astype(BF16)
    vvar_ref[2] = jnp.where(left, vswap, 0.0).astype(BF16)
    vvar_ref[3] = jnp.where(left, 0.0, v).astype(BF16)

    def proj_chunk(w_ref, col0):
        return _dot(h, w_ref[:, pl.ds(col0, PROJ_COLS)])

    def qm_task(col0):
        qm_ref[:, pl.ds(col0, PROJ_COLS)] = proj_chunk(wqm_ref, col0).astype(BF16)

    def mem_head_task(hd):
        cols = pl.ds(hd * MEM_HEAD_DIM, MEM_HEAD_DIM)
        s = _dot(qm_ref[:, cols], ktm_ref[0, cols, :])
        m = jnp.max(s, axis=-1, keepdims=True)
        p = jnp.exp(s - m)
        den = jnp.sum(p, axis=-1, keepdims=True)
        mo_ref[:, cols] = (_dot(p.astype(BF16), vm_ref[0, :, cols]) * (1.0 / den)).astype(BF16)

    def sgu_u_task():
        z = _dot(h, wz_ref[:, pl.ds(0, SGU_WIDTH)])
        u_ref[...] = z * _gelu_cdf(z)

    def sgu_v_task():
        z = _dot(h, wz_ref[:, pl.ds(SGU_WIDTH, SGU_WIDTH)])
        vz = z * _gelu_cdf(z)
        mu = jnp.mean(vz, axis=-1, keepdims=True)
        vc = vz - mu
        var = jnp.mean(vc * vc, axis=-1, keepdims=True)
        vln_ref[...] = (vc * lax.rsqrt(var + EPS) * lng_ref[...] + lnb_ref[...]).astype(BF16)

    def sgu_task(n):
        rows = pl.ds(n * SGU_CHUNK, SGU_CHUNK)
        for grp in range(SGU_GROUPS):
            cols = pl.ds(grp * LANES, LANES)
            sv = _dot(sguw_ref[grp], vln_ref[rows, cols]) + sgub_ref[:, cols]
            sgu_ref[rows, cols] = (u_ref[rows, cols] * sv).astype(BF16)

    def gate_task(br, col0):
        cols = pl.ds(br * D_MODEL + col0, PROJ_COLS)
        gate_ref[:, cols] = jax.nn.sigmoid(proj_chunk(wgl_ref, br * D_MODEL + col0))

    def branch_term(br, src_ref, w_ref, col0):
        gate = gate_ref[:, pl.ds(br * D_MODEL + col0, PROJ_COLS)]
        return gate * _dot(src_ref[...], w_ref[:, pl.ds(col0, PROJ_COLS)])

    def branch_task(br, src_ref, w_ref, first, col0):
        cols = pl.ds(col0, PROJ_COLS)
        term = branch_term(br, src_ref, w_ref, col0)
        merged_ref[:, cols] = term if first else merged_ref[:, cols] + term

    part = functools.partial
    col_chunks = range(0, D_MODEL, PROJ_COLS)
    tasks = ([part(qm_task, c0) for c0 in range(0, MEM_WIDTH, PROJ_COLS)]
             + [part(mem_head_task, hd) for hd in range(MEM_HEADS)]
             + [part(gate_task, 2, c0) for c0 in col_chunks]
             + [part(branch_task, 2, mo_ref, wbm_ref, True, c0) for c0 in col_chunks]
             + [sgu_u_task, sgu_v_task]
             + [part(sgu_task, n) for n in range(ts // SGU_CHUNK)]
             + [part(gate_task, 1, c0) for c0 in col_chunks]
             + [part(branch_task, 1, sgu_ref, wbs_ref, False, c0) for c0 in col_chunks]
             + [part(gate_task, 0, c0) for c0 in col_chunks])
    n_units = nq * N_KV_HEADS
    tasks_per_unit = -(-len(tasks) // n_units)

    def run_tasks(n):
        for _ in range(min(n, len(tasks))):
            tasks.pop(0)()

    q_ref[...] = _dot(h, wq_ref[...]).astype(BF16)
    row_half = lax.broadcasted_iota(jnp.int32, (2 * BLOCK, 1), 0) < BLOCK
    lane_left = lax.broadcasted_iota(jnp.int32, (2 * BLOCK, LANES), 1) < HEAD_DIM
    for j in range(nq):
        if j == 0:
            sel = jnp.where(i == 0, 1, 0)
        elif j == nq - 1:
            sel = jnp.where(i == nt - 1, 2, 0)
        else:
            sel = 0
        rows = pl.ds(j * BLOCK, BLOCK)
        keys = pl.ds(j * BLOCK, 3 * BLOCK)
        pairs = []
        for kvh in range(N_KV_HEADS):
            qs = jnp.concatenate(
                [q_ref[rows, pl.ds((2 * kvh) * LANES, LANES)],
                 q_ref[rows, pl.ds((2 * kvh + 1) * LANES, LANES)]], axis=0)
            kcat = jnp.concatenate([kvar_ref[2 * kvh, :, keys], kvar_ref[2 * kvh + 1, :, keys]], axis=1)
            s = jnp.dot(qs, kcat, preferred_element_type=F32) + bias_ref[sel, kvh]
            ps, invs = [], []
            for slot in range(2):
                sh = s[:, slot * 3 * BLOCK:(slot + 1) * 3 * BLOCK]
                snk = jnp.where(row_half, sink_ref[4 * kvh + slot], sink_ref[4 * kvh + 2 + slot])
                m = jnp.maximum(jnp.max(sh, axis=-1, keepdims=True), snk)
                p = jnp.exp(sh - m)
                den = jnp.sum(p, axis=-1, keepdims=True) + jnp.exp(snk - m)
                ps.append(p.astype(BF16))
                invs.append(1.0 / den)
            vcat = jnp.concatenate([vvar_ref[2 * kvh, keys, :], vvar_ref[2 * kvh + 1, keys, :]], axis=0)
            pv = jnp.dot(jnp.concatenate(ps, axis=1), vcat, preferred_element_type=F32)
            pv = pv * jnp.where(lane_left, invs[0], invs[1])
            pairs += [pv[:BLOCK], pv[BLOCK:]]
            run_tasks(tasks_per_unit)
        attn_ref[rows, :] = jnp.concatenate(pairs, axis=1).astype(BF16)
    run_tasks(len(tasks))

    for c0 in col_chunks:
        cols = pl.ds(c0, PROJ_COLS)
        mrg_ref[:, cols] = (merged_ref[:, cols] + branch_term(0, attn_ref, wba_ref, c0)).astype(BF16)
    o_ref[0] = x_ref[0] + _dot(mrg_ref[...], wo_ref[...])


def _mixer(x, ktm, vm, bucket, rel_bias, sink, g_mix, wq, wkv, wz, wqm, wgl,
           sguw, sgub, lng, lnb, wba, wbs, wbm, wo):
    B, S, D = x.shape
    M = vm.shape[1]
    ts = TILE
    assert S % ts == 0 and ts % BLOCK == 0 and ts // BLOCK >= 2 and S // BLOCK >= 2
    nt = S // ts
    r = ts // BLOCK
    nblk = S // BLOCK
    w = ts + 2 * BLOCK
    smem = pl.BlockSpec(memory_space=pltpu.SMEM)
    in_specs = [
        pl.BlockSpec((1, ts, D), lambda b, i: (b, i, 0)),
        pl.BlockSpec((1, BLOCK, D), lambda b, i: (b, jnp.maximum(i * r - 1, 0), 0)),
        pl.BlockSpec((1, BLOCK, D), lambda b, i: (b, jnp.minimum((i + 1) * r, nblk - 1), 0)),
        pl.BlockSpec((1, MEM_WIDTH, M), lambda b, i: (b, 0, 0)),
        pl.BlockSpec((1, M, MEM_WIDTH), lambda b, i: (b, 0, 0)),
        _const_spec(bucket.shape), smem, smem,
    ] + [_const_spec(a.shape) for a in (g_mix, wq, wkv, wz, wqm, wgl, sguw, sgub, lng, lnb,
                                        wba, wbs, wbm, wo)]
    scratch = [
        pltpu.VMEM((3, N_KV_HEADS, 2 * BLOCK, 6 * BLOCK), F32),
        pltpu.VMEM((w, D), BF16),
        pltpu.VMEM((4, KV_W, w), BF16),
        pltpu.VMEM((4, w, KV_W), BF16),
        pltpu.VMEM((ts, Q_W), BF16),
        pltpu.VMEM((ts, Q_W), BF16),
        pltpu.VMEM((ts, SGU_WIDTH), F32),
        pltpu.VMEM((ts, SGU_WIDTH), BF16),
        pltpu.VMEM((ts, SGU_WIDTH), BF16),
        pltpu.VMEM((ts, MEM_WIDTH), BF16),
        pltpu.VMEM((ts, MEM_WIDTH), BF16),
        pltpu.VMEM((ts, 3 * D_MODEL), F32),
        pltpu.VMEM((ts, D_MODEL), F32),
        pltpu.VMEM((ts, D_MODEL), BF16),
    ]
    return pl.pallas_call(
        _mixer_kernel,
        grid=(B, nt),
        in_specs=in_specs,
        out_specs=pl.BlockSpec((1, ts, D), lambda b, i: (b, i, 0)),
        out_shape=jax.ShapeDtypeStruct((B, S, D), F32),
        scratch_shapes=scratch,
        compiler_params=pltpu.CompilerParams(
            dimension_semantics=("arbitrary", "arbitrary"), vmem_limit_bytes=VMEM_LIMIT),
        name="mixer",
    )(x, x, x, ktm, vm, bucket, rel_bias, sink, g_mix, wq, wkv, wz, wqm, wgl,
      sguw, sgub, lng, lnb, wba, wbs, wbm, wo)


def _ffn_kernel(x_ref, xp_ref, xn_ref, g_ref, wup_ref, cw_ref, cb_ref, wdn_ref, gf_ref,
                o_ref, hext_ref, act_ref, *, final_norm):
    ts = x_ref.shape[1]
    pad = BF16_ROWS
    i = pl.program_id(1)
    nt = pl.num_programs(1)
    g = g_ref[...]
    hp = _rms(xp_ref[0], g) * jnp.where(i > 0, 1.0, 0.0)
    hn = _rms(xn_ref[0], g) * jnp.where(i < nt - 1, 1.0, 0.0)
    hext_ref[0:pad] = hp.astype(BF16)
    hext_ref[pad:pad + ts] = _rms(x_ref[0], g).astype(BF16)
    hext_ref[pad + ts:] = hn.astype(BF16)

    rows_ext = ts + 2 * pad

    def up(col0):
        return _dot(hext_ref[...], wup_ref[:, pl.ds(col0, FFN_COLS)])

    def conv(a, col0):
        cols = pl.ds(col0, FFN_COLS)
        prev = pltpu.roll(a, 1, axis=0)
        nxt = pltpu.roll(a, rows_ext - 1, axis=0)
        mid = slice(pad, pad + ts)
        return (prev[mid] * cw_ref[0:1, cols] + a[mid] * cw_ref[1:2, cols]
                + nxt[mid] * cw_ref[2:3, cols] + cb_ref[:, cols])

    n_chunks = D_FF // FFN_COLS
    y = x_ref[0]
    done = 0
    for c in range(n_chunks):
        a_gate = up(c * FFN_COLS)
        a_up = up(D_FF + c * FFN_COLS)
        if c in DOWN_FLUSH:
            k0, k1 = done * FFN_COLS, c * FFN_COLS
            y = y + _dot(act_ref[:, k0:k1], wdn_ref[k0:k1, :])
            done = c
        gate = conv(a_gate, c * FFN_COLS)
        act = (gate * conv(a_up, D_FF + c * FFN_COLS)) * _gelu_cdf(gate)
        act_ref[:, pl.ds(c * FFN_COLS, FFN_COLS)] = act.astype(BF16)
    y = y + _dot(act_ref[:, done * FFN_COLS:], wdn_ref[done * FFN_COLS:, :])
    o_ref[0] = _rms(y, gf_ref[...]) if final_norm else y


def _ffn(x, g_ffn, wup, conv_w, conv_b, wdn, g_final, final_norm):
    B, S, D = x.shape
    ts = FFN_TILE
    pad = BF16_ROWS
    assert S % ts == 0 and ts % pad == 0 and D_FF % FFN_COLS == 0
    nt = S // ts
    r = ts // pad
    nblk = S // pad
    in_specs = [
        pl.BlockSpec((1, ts, D), lambda b, i: (b, i, 0)),
        pl.BlockSpec((1, pad, D), lambda b, i: (b, jnp.maximum(i * r - 1, 0), 0)),
        pl.BlockSpec((1, pad, D), lambda b, i: (b, jnp.minimum((i + 1) * r, nblk - 1), 0)),
    ] + [_const_spec(a.shape) for a in (g_ffn, wup, conv_w, conv_b, wdn, g_final)]
    return pl.pallas_call(
        functools.partial(_ffn_kernel, final_norm=final_norm),
        grid=(B, nt),
        in_specs=in_specs,
        out_specs=pl.BlockSpec((1, ts, D), lambda b, i: (b, i, 0)),
        out_shape=jax.ShapeDtypeStruct((B, S, D), F32),
        scratch_shapes=[
            pltpu.VMEM((ts + 2 * pad, D), BF16),
            pltpu.VMEM((ts, D_FF), BF16),
        ],
        compiler_params=pltpu.CompilerParams(
            dimension_semantics=("arbitrary", "arbitrary"), vmem_limit_bytes=VMEM_LIMIT),
        name="ffn",
    )(x, x, x, g_ffn, wup, conv_w, conv_b, wdn, g_final)


def _t5_bucket(rel):
    nb = N_BUCKETS // 2
    max_exact = nb // 2
    ret = (rel > 0).astype(jnp.int32) * nb
    n = jnp.abs(rel)
    nf = jnp.maximum(n, 1).astype(jnp.float32)
    large = max_exact + (jnp.log(nf / max_exact) / math.log(MAX_DISTANCE / max_exact)
                         * (nb - max_exact)).astype(jnp.int32)
    large = jnp.minimum(large, nb - 1)
    return ret + jnp.where(n < max_exact, n, large)


def kernel(x_prompt, x_sample, mem_prompt, mem_sample, rel_bias, g_mix, w_in, attn_sink,
           sgu_w, sgu_b, sgu_ln_g, sgu_ln_b, g_mem, w_mem_kv, w_br_attn, w_br_sgu, w_br_mem,
           w_out, g_ffn, w_up, conv_w, conv_b, w_down, g_final):
    depth = w_in.shape[0]
    rel = (jnp.arange(3 * BLOCK)[None, :] - BLOCK) - jnp.arange(BLOCK)[:, None]
    bucket = _t5_bucket(rel)
    row = lambda a: a.reshape(1, -1)

    def encoder(x, mem):
        for l in range(depth):
            wi = w_in[l]
            wq = (wi[:, :OFF_K] * (HEAD_DIM ** -0.5)).astype(BF16)
            wkv = wi[:, OFF_K:OFF_Z].astype(BF16)
            wz = wi[:, OFF_Z:OFF_QM].astype(BF16)
            wqm = wi[:, OFF_QM:OFF_GL].astype(BF16)
            wgl = wi[:, OFF_GL:].astype(BF16)
            sgub = jnp.repeat(sgu_b[l].T, SGU_WIDTH // SGU_GROUPS, axis=1)
            ktm, vm = _mem_kv(mem, row(g_mem[l]), w_mem_kv[l].astype(BF16))
            x = _mixer(x, ktm, vm, bucket, rel_bias, attn_sink[l], row(g_mix[l]),
                       wq, wkv, wz, wqm, wgl, sgu_w[l].astype(BF16), sgub,
                       row(sgu_ln_g[l]), row(sgu_ln_b[l]),
                       w_br_attn[l].astype(BF16), w_br_sgu[l].astype(BF16),
                       w_br_mem[l].astype(BF16), w_out[l].astype(BF16))
            x = _ffn(x, row(g_ffn[l]), w_up[l].astype(BF16), conv_w[l], row(conv_b[l]),
                     w_down[l].astype(BF16), row(g_final), final_norm=(l == depth - 1))
        return x

    return encoder(x_prompt, mem_prompt), encoder(x_sample, mem_sample)
```

```python
import functools
import math

import jax
import jax.numpy as jnp
from jax import lax
from jax.experimental import pallas as pl
from jax.experimental.pallas import tpu as pltpu

D_MODEL = 1024
HEAD_DIM = 64
N_HEADS = 8
N_KV_HEADS = 2
WINDOW = 128
BLOCK = 128
N_BUCKETS = 32
MAX_DISTANCE = 128
SGU_WIDTH = 512
SGU_GROUPS = 4
SGU_CHUNK = 128
MEM_HEADS = 4
MEM_HEAD_DIM = 128
MEM_WIDTH = MEM_HEADS * MEM_HEAD_DIM
D_FF = 2816
EPS = 1e-6
NEG = -1e30

Q_W = N_HEADS * HEAD_DIM
KV_W = N_KV_HEADS * HEAD_DIM
OFF_K = Q_W
OFF_Z = Q_W + 2 * KV_W
OFF_QM = OFF_Z + 2 * SGU_WIDTH
OFF_GL = OFF_QM + MEM_WIDTH

LANES = 128
BF16_ROWS = 16
VMEM_LIMIT = 56 * 1024 * 1024

TILE = 512
FFN_COLS = 256
DOWN_PIECE = 4
PROJ_COLS = 256

BF16 = jnp.bfloat16
F32 = jnp.float32


def _rms(x, g=None):
    y = x * lax.rsqrt(jnp.mean(x * x, axis=-1, keepdims=True) + EPS)
    return y if g is None else y * g


_GELU_C0 = math.sqrt(2.0 / math.pi)
_GELU_C1 = _GELU_C0 * 0.044715


def _gelu_cdf(x):
    return 0.5 + 0.5 * jnp.tanh(x * (_GELU_C0 + _GELU_C1 * (x * x)))


def _dot(a, b):
    return jnp.dot(a, b, preferred_element_type=F32)


def _const_spec(shape):
    nd = len(shape)
    return pl.BlockSpec(shape, lambda *_: (0,) * nd, pipeline_mode=pl.Buffered(1))


def _mem_kv_kernel(mem_ref, w_ref, kt_ref, v_ref):
    h = _rms(mem_ref[0]).astype(BF16)
    kv = _dot(h, w_ref[...])
    k = kv[:, :MEM_WIDTH] * (MEM_HEAD_DIM ** -0.5)
    kt_ref[0] = k.T.astype(BF16)
    v_ref[0] = kv[:, MEM_WIDTH:].astype(BF16)


def _mem_kv(mem, w_mem_kv):
    B, M, D = mem.shape
    return pl.pallas_call(
        _mem_kv_kernel,
        grid=(B,),
        in_specs=[
            pl.BlockSpec((1, M, D), lambda b: (b, 0, 0)),
            _const_spec((D, 2 * MEM_WIDTH)),
        ],
        out_specs=[
            pl.BlockSpec((1, MEM_WIDTH, M), lambda b: (b, 0, 0)),
            pl.BlockSpec((1, M, MEM_WIDTH), lambda b: (b, 0, 0)),
        ],
        out_shape=[
            jax.ShapeDtypeStruct((B, MEM_WIDTH, M), BF16),
            jax.ShapeDtypeStruct((B, M, MEM_WIDTH), BF16),
        ],
        compiler_params=pltpu.CompilerParams(
            dimension_semantics=("arbitrary",), vmem_limit_bytes=VMEM_LIMIT),
        name="mem_kv",
    )(mem, w_mem_kv)


def _mixer_kernel(x_ref, xp_ref, xn_ref, ktm_ref, vm_ref, bucket_ref, relb_ref, sink_ref,
                  wq_ref, wkv_ref, wz_ref, wqm_ref, wgl_ref,
                  sguw_ref, sgub_ref, lng_ref, lnb_ref,
                  wba_ref, wbs_ref, wbm_ref, wo_ref,
                  o_ref,
                  bias_ref, hext_ref, kvar_ref, vvar_ref, q_ref, attn_ref,
                  u_ref, vg_ref, vln_ref, sgu_ref, qm_ref, mo_ref, gate_ref):
    ts = x_ref.shape[1]
    nq = ts // BLOCK
    i = pl.program_id(1)
    nt = pl.num_programs(1)
    first_step = jnp.logical_and(pl.program_id(0) == 0, i == 0)

    @pl.when(first_step)
    def _():
        bk = bucket_ref[...]
        qi = lax.broadcasted_iota(jnp.int32, (BLOCK, 3 * BLOCK), 0)
        kj = lax.broadcasted_iota(jnp.int32, (BLOCK, 3 * BLOCK), 1)
        in_window = jnp.abs(kj - BLOCK - qi) <= WINDOW
        for hd in range(N_HEADS):
            acc = jnp.zeros((BLOCK, 3 * BLOCK), F32)
            for b in range(N_BUCKETS):
                acc = jnp.where(bk == b, relb_ref[b, hd], acc)
            base = jnp.where(in_window, acc, NEG)
            kv, rem = divmod(hd, N_HEADS // N_KV_HEADS)
            half, slot = divmod(rem, 2)
            c = kv * 2 + slot
            rows = pl.ds(half * BLOCK, BLOCK)
            bias_ref[0, c, rows, :] = base
            bias_ref[1, c, rows, :] = jnp.where(kj < BLOCK, NEG, base)
            bias_ref[2, c, rows, :] = jnp.where(kj >= 2 * BLOCK, NEG, base)

    hext_ref[0:BLOCK] = _rms(xp_ref[0]).astype(BF16)
    hext_ref[BLOCK:BLOCK + ts] = _rms(x_ref[0]).astype(BF16)
    hext_ref[BLOCK + ts:] = _rms(xn_ref[0]).astype(BF16)
    h = hext_ref[BLOCK:BLOCK + ts]

    w = ts + 2 * BLOCK
    kv_all = _dot(hext_ref[...], wkv_ref[...])
    kt = kv_all[:, :KV_W].T.astype(BF16)
    v = kv_all[:, KV_W:]
    vswap = pltpu.roll(v, HEAD_DIM, axis=1)
    left = lax.broadcasted_iota(jnp.int32, (w, KV_W), 1) < HEAD_DIM
    zrow = jnp.zeros((HEAD_DIM, w), BF16)
    for kvh in range(N_KV_HEADS):
        kk = kt[kvh * HEAD_DIM:(kvh + 1) * HEAD_DIM]
        kvar_ref[kvh * 2 + 0] = jnp.concatenate([kk, zrow], axis=0)
        kvar_ref[kvh * 2 + 1] = jnp.concatenate([zrow, kk], axis=0)
    vvar_ref[0] = jnp.where(left, v, 0.0).astype(BF16)
    vvar_ref[1] = jnp.where(left, 0.0, vswap).astype(BF16)
    vvar_ref[2] = jnp.where(left, vswap, 0.0).astype(BF16)
    vvar_ref[3] = jnp.where(left, 0.0, v).astype(BF16)

    def proj_chunk(w_ref, col0):
        return _dot(h, w_ref[:, pl.ds(col0, PROJ_COLS)])

    def qm_task(col0):
        qm_ref[:, pl.ds(col0, PROJ_COLS)] = proj_chunk(wqm_ref, col0).astype(BF16)

    def z_task(col0):
        z = proj_chunk(wz_ref, col0)
        z = z * _gelu_cdf(z)
        if col0 < SGU_WIDTH:
            u_ref[:, pl.ds(col0, PROJ_COLS)] = z
        else:
            vg_ref[:, pl.ds(col0 - SGU_WIDTH, PROJ_COLS)] = z

    def gate_task(col0):
        gate_ref[:, pl.ds(col0, PROJ_COLS)] = jax.nn.sigmoid(proj_chunk(wgl_ref, col0))

    tasks = ([functools.partial(qm_task, c0) for c0 in range(0, MEM_WIDTH, PROJ_COLS)]
             + [functools.partial(z_task, c0) for c0 in range(0, 2 * SGU_WIDTH, PROJ_COLS)]
             + [functools.partial(gate_task, c0) for c0 in range(0, 3 * D_MODEL, PROJ_COLS)])

    def run_task():
        if tasks:
            tasks.pop(0)()

    q_ref[...] = _dot(h, wq_ref[...]).astype(BF16)
    row_half = lax.broadcasted_iota(jnp.int32, (2 * BLOCK, 1), 0) < BLOCK
    for j in range(nq):
        if j == 0:
            sel = jnp.where(i == 0, 1, 0)
        elif j == nq - 1:
            sel = jnp.where(i == nt - 1, 2, 0)
        else:
            sel = 0
        rows = pl.ds(j * BLOCK, BLOCK)
        keys = pl.ds(j * BLOCK, 3 * BLOCK)
        pairs = []
        for kvh in range(N_KV_HEADS):
            qs = jnp.concatenate(
                [q_ref[rows, pl.ds((2 * kvh) * LANES, LANES)],
                 q_ref[rows, pl.ds((2 * kvh + 1) * LANES, LANES)]], axis=0)
            acc = None
            for slot in range(2):
                c = kvh * 2 + slot
                s = _dot(qs, kvar_ref[c, :, keys]) + bias_ref[sel, c]
                snk = jnp.where(row_half, sink_ref[4 * kvh + slot], sink_ref[4 * kvh + 2 + slot])
                m = jnp.maximum(jnp.max(s, axis=-1, keepdims=True), snk)
                p = jnp.exp(s - m)
                den = jnp.sum(p, axis=-1, keepdims=True) + jnp.exp(snk - m)
                pv = _dot(p.astype(BF16), vvar_ref[c, keys, :]) * (1.0 / den)
                acc = pv if acc is None else acc + pv
                run_task()
            pairs += [acc[:BLOCK], acc[BLOCK:]]
        attn_ref[rows, :] = jnp.concatenate(pairs, axis=1).astype(BF16)

    for hd in range(MEM_HEADS):
        cols = pl.ds(hd * MEM_HEAD_DIM, MEM_HEAD_DIM)
        s = _dot(qm_ref[:, cols], ktm_ref[0, cols, :])
        m = jnp.max(s, axis=-1, keepdims=True)
        p = jnp.exp(s - m)
        den = jnp.sum(p, axis=-1, keepdims=True)
        mo_ref[:, cols] = (_dot(p.astype(BF16), vm_ref[0, :, cols]) * (1.0 / den)).astype(BF16)
        run_task()
    while tasks:
        run_task()

    vz = vg_ref[...]
    mu = jnp.mean(vz, axis=-1, keepdims=True)
    vc = vz - mu
    var = jnp.mean(vc * vc, axis=-1, keepdims=True)
    vln_ref[...] = (vc * lax.rsqrt(var + EPS) * lng_ref[...] + lnb_ref[...]).astype(BF16)
    for n in range(ts // SGU_CHUNK):
        rows = pl.ds(n * SGU_CHUNK, SGU_CHUNK)
        for grp in range(SGU_GROUPS):
            cols = pl.ds(grp * LANES, LANES)
            sv = _dot(sguw_ref[grp], vln_ref[rows, cols]) + sgub_ref[:, cols]
            sgu_ref[rows, cols] = (u_ref[rows, cols] * sv).astype(BF16)

    merged = None
    for br, (src_ref, w_ref) in enumerate(((attn_ref, wba_ref), (sgu_ref, wbs_ref), (mo_ref, wbm_ref))):
        term = gate_ref[:, pl.ds(br * D_MODEL, D_MODEL)] * _dot(src_ref[...], w_ref[...])
        merged = term if merged is None else merged + term
    o_ref[0] = x_ref[0] + _dot(merged.astype(BF16), wo_ref[...])


def _mixer(x, ktm, vm, bucket, rel_bias, sink, wq, wkv, wz, wqm, wgl,
           sguw, sgub, lng, lnb, wba, wbs, wbm, wo):
    B, S, D = x.shape
    M = vm.shape[1]
    ts = TILE
    assert S % ts == 0 and ts % BLOCK == 0 and ts // BLOCK >= 2 and S // BLOCK >= 2
    nt = S // ts
    r = ts // BLOCK
    nblk = S // BLOCK
    w = ts + 2 * BLOCK
    smem = pl.BlockSpec(memory_space=pltpu.SMEM)
    in_specs = [
        pl.BlockSpec((1, ts, D), lambda b, i: (b, i, 0)),
        pl.BlockSpec((1, BLOCK, D), lambda b, i: (b, jnp.maximum(i * r - 1, 0), 0)),
        pl.BlockSpec((1, BLOCK, D), lambda b, i: (b, jnp.minimum((i + 1) * r, nblk - 1), 0)),
        pl.BlockSpec((1, MEM_WIDTH, M), lambda b, i: (b, 0, 0)),
        pl.BlockSpec((1, M, MEM_WIDTH), lambda b, i: (b, 0, 0)),
        _const_spec(bucket.shape), smem, smem,
    ] + [_const_spec(a.shape) for a in (wq, wkv, wz, wqm, wgl, sguw, sgub, lng, lnb,
                                        wba, wbs, wbm, wo)]
    scratch = [
        pltpu.VMEM((3, 4, 2 * BLOCK, 3 * BLOCK), F32),
        pltpu.VMEM((w, D), BF16),
        pltpu.VMEM((4, KV_W, w), BF16),
        pltpu.VMEM((4, w, KV_W), BF16),
        pltpu.VMEM((ts, Q_W), BF16),
        pltpu.VMEM((ts, Q_W), BF16),
        pltpu.VMEM((ts, SGU_WIDTH), F32),
        pltpu.VMEM((ts, SGU_WIDTH), F32),
        pltpu.VMEM((ts, SGU_WIDTH), BF16),
        pltpu.VMEM((ts, SGU_WIDTH), BF16),
        pltpu.VMEM((ts, MEM_WIDTH), BF16),
        pltpu.VMEM((ts, MEM_WIDTH), BF16),
        pltpu.VMEM((ts, 3 * D_MODEL), F32),
    ]
    operands = (x, x, x, ktm, vm, bucket, rel_bias, sink, wq, wkv, wz, wqm, wgl,
                sguw, sgub, lng, lnb, wba, wbs, wbm, wo)
    return pl.pallas_call(
        _mixer_kernel,
        grid=(B, nt),
        in_specs=in_specs,
        out_specs=pl.BlockSpec((1, ts, D), lambda b, i: (b, i, 0)),
        out_shape=jax.ShapeDtypeStruct((B, S, D), F32),
        scratch_shapes=scratch,
        compiler_params=pltpu.CompilerParams(
            dimension_semantics=("arbitrary", "arbitrary"), vmem_limit_bytes=VMEM_LIMIT),
        name="mixer",
    )(*operands)


def _ffn_kernel(x_ref, xp_ref, xn_ref, wup_ref, cw_ref, cb_ref, wdn_ref, gf_ref,
                o_ref, hext_ref, act_ref, *, final_norm):
    ts = x_ref.shape[1]
    pad = BF16_ROWS
    i = pl.program_id(1)
    nt = pl.num_programs(1)
    hp = _rms(xp_ref[0]) * jnp.where(i > 0, 1.0, 0.0)
    hn = _rms(xn_ref[0]) * jnp.where(i < nt - 1, 1.0, 0.0)
    hext_ref[0:pad] = hp.astype(BF16)
    hext_ref[pad:pad + ts] = _rms(x_ref[0]).astype(BF16)
    hext_ref[pad + ts:] = hn.astype(BF16)

    rows_ext = ts + 2 * pad

    def up(col0):
        return _dot(hext_ref[...], wup_ref[:, pl.ds(col0, FFN_COLS)])

    def conv(a, col0):
        cols = pl.ds(col0, FFN_COLS)
        prev = pltpu.roll(a, 1, axis=0)
        nxt = pltpu.roll(a, rows_ext - 1, axis=0)
        mid = slice(pad, pad + ts)
        return (prev[mid] * cw_ref[0:1, cols] + a[mid] * cw_ref[1:2, cols]
                + nxt[mid] * cw_ref[2:3, cols] + cb_ref[:, cols])

    n_chunks = D_FF // FFN_COLS
    y = x_ref[0]
    done = 0
    for c in range(n_chunks):
        a_gate = up(c * FFN_COLS)
        a_up = up(D_FF + c * FFN_COLS)
        if c > 0 and (c % DOWN_PIECE == 0):
            k0, k1 = done * FFN_COLS, c * FFN_COLS
            y = y + _dot(act_ref[:, k0:k1], wdn_ref[k0:k1, :])
            done = c
        gate = conv(a_gate, c * FFN_COLS)
        act = (gate * conv(a_up, D_FF + c * FFN_COLS)) * _gelu_cdf(gate)
        act_ref[:, pl.ds(c * FFN_COLS, FFN_COLS)] = act.astype(BF16)
    y = y + _dot(act_ref[:, done * FFN_COLS:], wdn_ref[done * FFN_COLS:, :])
    o_ref[0] = _rms(y, gf_ref[...]) if final_norm else y


def _ffn(x, wup, conv_w, conv_b, wdn, g_final, final_norm):
    B, S, D = x.shape
    ts = TILE
    pad = BF16_ROWS
    assert S % ts == 0 and ts % pad == 0 and D_FF % FFN_COLS == 0
    nt = S // ts
    r = ts // pad
    nblk = S // pad
    in_specs = [
        pl.BlockSpec((1, ts, D), lambda b, i: (b, i, 0)),
        pl.BlockSpec((1, pad, D), lambda b, i: (b, jnp.maximum(i * r - 1, 0), 0)),
        pl.BlockSpec((1, pad, D), lambda b, i: (b, jnp.minimum((i + 1) * r, nblk - 1), 0)),
    ] + [_const_spec(a.shape) for a in (wup, conv_w, conv_b, wdn, g_final)]
    operands = (x, x, x, wup, conv_w, conv_b, wdn, g_final)
    return pl.pallas_call(
        functools.partial(_ffn_kernel, final_norm=final_norm),
        grid=(B, nt),
        in_specs=in_specs,
        out_specs=pl.BlockSpec((1, ts, D), lambda b, i: (b, i, 0)),
        out_shape=jax.ShapeDtypeStruct((B, S, D), F32),
        scratch_shapes=[
            pltpu.VMEM((ts + 2 * pad, D), BF16),
            pltpu.VMEM((ts, D_FF), BF16),
        ],
        compiler_params=pltpu.CompilerParams(
            dimension_semantics=("arbitrary", "arbitrary"), vmem_limit_bytes=VMEM_LIMIT),
        name="ffn",
    )(*operands)


def _t5_bucket(rel):
    nb = N_BUCKETS // 2
    max_exact = nb // 2
    ret = (rel > 0).astype(jnp.int32) * nb
    n = jnp.abs(rel)
    nf = jnp.maximum(n, 1).astype(jnp.float32)
    large = max_exact + (jnp.log(nf / max_exact) / math.log(MAX_DISTANCE / max_exact)
                         * (nb - max_exact)).astype(jnp.int32)
    large = jnp.minimum(large, nb - 1)
    return ret + jnp.where(n < max_exact, n, large)


def kernel(x_prompt, x_sample, mem_prompt, mem_sample, rel_bias, g_mix, w_in, attn_sink,
           sgu_w, sgu_b, sgu_ln_g, sgu_ln_b, g_mem, w_mem_kv, w_br_attn, w_br_sgu, w_br_mem,
           w_out, g_ffn, w_up, conv_w, conv_b, w_down, g_final):
    depth = w_in.shape[0]
    rel = (jnp.arange(3 * BLOCK)[None, :] - BLOCK) - jnp.arange(BLOCK)[:, None]
    bucket = _t5_bucket(rel)
    row = lambda a: a.reshape(1, -1)

    def encoder(x, mem):
        for l in range(depth):
            wi = w_in[l] * g_mix[l][:, None]
            wq = (wi[:, :OFF_K] * (HEAD_DIM ** -0.5)).astype(BF16)
            wkv = wi[:, OFF_K:OFF_Z].astype(BF16)
            wz = wi[:, OFF_Z:OFF_QM].astype(BF16)
            wqm = wi[:, OFF_QM:OFF_GL].astype(BF16)
            wgl = wi[:, OFF_GL:].astype(BF16)
            sgub = jnp.repeat(sgu_b[l].T, SGU_WIDTH // SGU_GROUPS, axis=1)
            ktm, vm = _mem_kv(mem, (w_mem_kv[l] * g_mem[l][:, None]).astype(BF16))
            x = _mixer(x, ktm, vm, bucket, rel_bias, attn_sink[l],
                       wq, wkv, wz, wqm, wgl, sgu_w[l].astype(BF16), sgub,
                       row(sgu_ln_g[l]), row(sgu_ln_b[l]),
                       w_br_attn[l].astype(BF16), w_br_sgu[l].astype(BF16),
                       w_br_mem[l].astype(BF16), w_out[l].astype(BF16))
            x = _ffn(x, (w_up[l] * g_ffn[l][:, None]).astype(BF16), conv_w[l], row(conv_b[l]),
                     w_down[l].astype(BF16), row(g_final), final_norm=(l == depth - 1))
        return x

    return encoder(x_prompt, mem_prompt), encoder(x_sample, mem_sample)
```

```python
import functools
import math

import jax
import jax.numpy as jnp
from jax import lax
from jax.experimental import pallas as pl
from jax.experimental.pallas import tpu as pltpu

D_MODEL = 1024
HEAD_DIM = 64
N_HEADS = 8
N_KV_HEADS = 2
WINDOW = 128
BLOCK = 128
N_BUCKETS = 32
MAX_DISTANCE = 128
SGU_WIDTH = 512
SGU_GROUPS = 4
SGU_CHUNK = 128
MEM_HEADS = 4
MEM_HEAD_DIM = 128
MEM_WIDTH = MEM_HEADS * MEM_HEAD_DIM
D_FF = 2816
EPS = 1e-6
NEG = -1e30

Q_W = N_HEADS * HEAD_DIM
KV_W = N_KV_HEADS * HEAD_DIM
OFF_K = Q_W
OFF_Z = Q_W + 2 * KV_W
OFF_QM = OFF_Z + 2 * SGU_WIDTH
OFF_GL = OFF_QM + MEM_WIDTH

LANES = 128
BF16_ROWS = 16
VMEM_LIMIT = 56 * 1024 * 1024

TILE = 512
FFN_COLS = 256
DOWN_PIECE = 4
PROJ_COLS = 256

BF16 = jnp.bfloat16
F32 = jnp.float32


def _rms(x, g):
    return x * lax.rsqrt(jnp.mean(x * x, axis=-1, keepdims=True) + EPS) * g


_GELU_C0 = math.sqrt(2.0 / math.pi)
_GELU_C1 = _GELU_C0 * 0.044715


def _gelu_cdf(x):
    return 0.5 + 0.5 * jnp.tanh(x * (_GELU_C0 + _GELU_C1 * (x * x)))


def _dot(a, b):
    return jnp.dot(a, b, preferred_element_type=F32)


def _const_spec(shape):
    nd = len(shape)
    return pl.BlockSpec(shape, lambda *_: (0,) * nd, pipeline_mode=pl.Buffered(1))


def _mem_kv_kernel(mem_ref, g_ref, w_ref, kt_ref, v_ref):
    h = _rms(mem_ref[0], g_ref[...]).astype(BF16)
    kv = _dot(h, w_ref[...])
    k = kv[:, :MEM_WIDTH] * (MEM_HEAD_DIM ** -0.5)
    kt_ref[0] = k.T.astype(BF16)
    v_ref[0] = kv[:, MEM_WIDTH:].astype(BF16)


def _mem_kv(mem, g_mem, w_mem_kv):
    B, M, D = mem.shape
    return pl.pallas_call(
        _mem_kv_kernel,
        grid=(B,),
        in_specs=[
            pl.BlockSpec((1, M, D), lambda b: (b, 0, 0)),
            _const_spec((1, D)),
            _const_spec((D, 2 * MEM_WIDTH)),
        ],
        out_specs=[
            pl.BlockSpec((1, MEM_WIDTH, M), lambda b: (b, 0, 0)),
            pl.BlockSpec((1, M, MEM_WIDTH), lambda b: (b, 0, 0)),
        ],
        out_shape=[
            jax.ShapeDtypeStruct((B, MEM_WIDTH, M), BF16),
            jax.ShapeDtypeStruct((B, M, MEM_WIDTH), BF16),
        ],
        compiler_params=pltpu.CompilerParams(
            dimension_semantics=("arbitrary",), vmem_limit_bytes=VMEM_LIMIT),
        name="mem_kv",
    )(mem, g_mem, w_mem_kv)


def _mixer_kernel(x_ref, xp_ref, xn_ref, ktm_ref, vm_ref, bucket_ref, relb_ref, sink_ref,
                  g_ref, wq_ref, wkv_ref, wz_ref, wqm_ref, wgl_ref,
                  sguw_ref, sgub_ref, lng_ref, lnb_ref,
                  wba_ref, wbs_ref, wbm_ref, wo_ref,
                  o_ref,
                  bias_ref, hext_ref, kvar_ref, vvar_ref, q_ref, attn_ref,
                  u_ref, vg_ref, vln_ref, sgu_ref, qm_ref, mo_ref, gate_ref):
    ts = x_ref.shape[1]
    nq = ts // BLOCK
    i = pl.program_id(1)
    nt = pl.num_programs(1)
    first_step = jnp.logical_and(pl.program_id(0) == 0, i == 0)

    @pl.when(first_step)
    def _():
        bk = bucket_ref[...]
        qi = lax.broadcasted_iota(jnp.int32, (BLOCK, 3 * BLOCK), 0)
        kj = lax.broadcasted_iota(jnp.int32, (BLOCK, 3 * BLOCK), 1)
        in_window = jnp.abs(kj - BLOCK - qi) <= WINDOW
        for hd in range(N_HEADS):
            acc = jnp.zeros((BLOCK, 3 * BLOCK), F32)
            for b in range(N_BUCKETS):
                acc = jnp.where(bk == b, relb_ref[b, hd], acc)
            base = jnp.where(in_window, acc, NEG)
            kv, rem = divmod(hd, N_HEADS // N_KV_HEADS)
            half, slot = divmod(rem, 2)
            c = kv * 2 + slot
            rows = pl.ds(half * BLOCK, BLOCK)
            bias_ref[0, c, rows, :] = base
            bias_ref[1, c, rows, :] = jnp.where(kj < BLOCK, NEG, base)
            bias_ref[2, c, rows, :] = jnp.where(kj >= 2 * BLOCK, NEG, base)

    g = g_ref[...]
    hext_ref[0:BLOCK] = _rms(xp_ref[0], g).astype(BF16)
    hext_ref[BLOCK:BLOCK + ts] = _rms(x_ref[0], g).astype(BF16)
    hext_ref[BLOCK + ts:] = _rms(xn_ref[0], g).astype(BF16)
    h = hext_ref[BLOCK:BLOCK + ts]

    w = ts + 2 * BLOCK
    kv_all = _dot(hext_ref[...], wkv_ref[...])
    kt = kv_all[:, :KV_W].T.astype(BF16)
    v = kv_all[:, KV_W:]
    vswap = pltpu.roll(v, HEAD_DIM, axis=1)
    left = lax.broadcasted_iota(jnp.int32, (w, KV_W), 1) < HEAD_DIM
    zrow = jnp.zeros((HEAD_DIM, w), BF16)
    for kvh in range(N_KV_HEADS):
        kk = kt[kvh * HEAD_DIM:(kvh + 1) * HEAD_DIM]
        kvar_ref[kvh * 2 + 0] = jnp.concatenate([kk, zrow], axis=0)
        kvar_ref[kvh * 2 + 1] = jnp.concatenate([zrow, kk], axis=0)
    vvar_ref[0] = jnp.where(left, v, 0.0).astype(BF16)
    vvar_ref[1] = jnp.where(left, 0.0, vswap).astype(BF16)
    vvar_ref[2] = jnp.where(left, vswap, 0.0).astype(BF16)
    vvar_ref[3] = jnp.where(left, 0.0, v).astype(BF16)

    def proj_chunk(w_ref, col0):
        return _dot(h, w_ref[:, pl.ds(col0, PROJ_COLS)])

    def qm_task(col0):
        qm_ref[:, pl.ds(col0, PROJ_COLS)] = proj_chunk(wqm_ref, col0).astype(BF16)

    def z_task(col0):
        z = proj_chunk(wz_ref, col0)
        z = z * _gelu_cdf(z)
        if col0 < SGU_WIDTH:
            u_ref[:, pl.ds(col0, PROJ_COLS)] = z
        else:
            vg_ref[:, pl.ds(col0 - SGU_WIDTH, PROJ_COLS)] = z

    def gate_task(col0):
        gate_ref[:, pl.ds(col0, PROJ_COLS)] = jax.nn.sigmoid(proj_chunk(wgl_ref, col0))

    tasks = ([functools.partial(qm_task, c0) for c0 in range(0, MEM_WIDTH, PROJ_COLS)]
             + [functools.partial(z_task, c0) for c0 in range(0, 2 * SGU_WIDTH, PROJ_COLS)]
             + [functools.partial(gate_task, c0) for c0 in range(0, 3 * D_MODEL, PROJ_COLS)])

    def run_task():
        if tasks:
            tasks.pop(0)()

    q_ref[...] = _dot(h, wq_ref[...]).astype(BF16)
    row_half = lax.broadcasted_iota(jnp.int32, (2 * BLOCK, 1), 0) < BLOCK
    for j in range(nq):
        if j == 0:
            sel = jnp.where(i == 0, 1, 0)
        elif j == nq - 1:
            sel = jnp.where(i == nt - 1, 2, 0)
        else:
            sel = 0
        rows = pl.ds(j * BLOCK, BLOCK)
        keys = pl.ds(j * BLOCK, 3 * BLOCK)
        combos = [(kvh, slot) for kvh in range(N_KV_HEADS) for slot in range(2)]
        scores = []
        for kvh, slot in combos:
            qs = jnp.concatenate(
                [q_ref[rows, pl.ds((2 * kvh) * LANES, LANES)],
                 q_ref[rows, pl.ds((2 * kvh + 1) * LANES, LANES)]], axis=0)
            scores.append(_dot(qs, kvar_ref[kvh * 2 + slot, :, keys]))
            run_task()
        probs = []
        for (kvh, slot), s in zip(combos, scores):
            s = s + bias_ref[sel, kvh * 2 + slot]
            snk = jnp.where(row_half, sink_ref[4 * kvh + slot], sink_ref[4 * kvh + 2 + slot])
            m = jnp.maximum(jnp.max(s, axis=-1, keepdims=True), snk)
            p = jnp.exp(s - m)
            den = jnp.sum(p, axis=-1, keepdims=True) + jnp.exp(snk - m)
            probs.append((p.astype(BF16), 1.0 / den))
        outs = [_dot(p, vvar_ref[kvh * 2 + slot, keys, :]) * inv
                for (kvh, slot), (p, inv) in zip(combos, probs)]
        pairs = []
        for kvh in range(N_KV_HEADS):
            acc = outs[2 * kvh] + outs[2 * kvh + 1]
            pairs += [acc[:BLOCK], acc[BLOCK:]]
        attn_ref[rows, :] = jnp.concatenate(pairs, axis=1).astype(BF16)

    for hd in range(MEM_HEADS):
        cols = pl.ds(hd * MEM_HEAD_DIM, MEM_HEAD_DIM)
        s = _dot(qm_ref[:, cols], ktm_ref[0, cols, :])
        m = jnp.max(s, axis=-1, keepdims=True)
        p = jnp.exp(s - m)
        den = jnp.sum(p, axis=-1, keepdims=True)
        mo_ref[:, cols] = (_dot(p.astype(BF16), vm_ref[0, :, cols]) * (1.0 / den)).astype(BF16)
        run_task()
    while tasks:
        run_task()

    vz = vg_ref[...]
    mu = jnp.mean(vz, axis=-1, keepdims=True)
    vc = vz - mu
    var = jnp.mean(vc * vc, axis=-1, keepdims=True)
    vln_ref[...] = (vc * lax.rsqrt(var + EPS) * lng_ref[...] + lnb_ref[...]).astype(BF16)
    for n in range(ts // SGU_CHUNK):
        rows = pl.ds(n * SGU_CHUNK, SGU_CHUNK)
        for grp in range(SGU_GROUPS):
            cols = pl.ds(grp * LANES, LANES)
            sv = _dot(sguw_ref[grp], vln_ref[rows, cols]) + sgub_ref[:, cols]
            sgu_ref[rows, cols] = (u_ref[rows, cols] * sv).astype(BF16)

    merged = None
    for br, (src_ref, w_ref) in enumerate(((attn_ref, wba_ref), (sgu_ref, wbs_ref), (mo_ref, wbm_ref))):
        term = gate_ref[:, pl.ds(br * D_MODEL, D_MODEL)] * _dot(src_ref[...], w_ref[...])
        merged = term if merged is None else merged + term
    o_ref[0] = x_ref[0] + _dot(merged.astype(BF16), wo_ref[...])


def _mixer(x, ktm, vm, bucket, rel_bias, sink, g_mix, wq, wkv, wz, wqm, wgl,
           sguw, sgub, lng, lnb, wba, wbs, wbm, wo):
    B, S, D = x.shape
    M = vm.shape[1]
    ts = TILE
    assert S % ts == 0 and ts % BLOCK == 0 and ts // BLOCK >= 2 and S // BLOCK >= 2
    nt = S // ts
    r = ts // BLOCK
    nblk = S // BLOCK
    w = ts + 2 * BLOCK
    smem = pl.BlockSpec(memory_space=pltpu.SMEM)
    in_specs = [
        pl.BlockSpec((1, ts, D), lambda b, i: (b, i, 0)),
        pl.BlockSpec((1, BLOCK, D), lambda b, i: (b, jnp.maximum(i * r - 1, 0), 0)),
        pl.BlockSpec((1, BLOCK, D), lambda b, i: (b, jnp.minimum((i + 1) * r, nblk - 1), 0)),
        pl.BlockSpec((1, MEM_WIDTH, M), lambda b, i: (b, 0, 0)),
        pl.BlockSpec((1, M, MEM_WIDTH), lambda b, i: (b, 0, 0)),
        _const_spec(bucket.shape), smem, smem,
    ] + [_const_spec(a.shape) for a in (g_mix, wq, wkv, wz, wqm, wgl, sguw, sgub, lng, lnb,
                                        wba, wbs, wbm, wo)]
    scratch = [
        pltpu.VMEM((3, 4, 2 * BLOCK, 3 * BLOCK), F32),
        pltpu.VMEM((w, D), BF16),
        pltpu.VMEM((4, KV_W, w), BF16),
        pltpu.VMEM((4, w, KV_W), BF16),
        pltpu.VMEM((ts, Q_W), BF16),
        pltpu.VMEM((ts, Q_W), BF16),
        pltpu.VMEM((ts, SGU_WIDTH), F32),
        pltpu.VMEM((ts, SGU_WIDTH), F32),
        pltpu.VMEM((ts, SGU_WIDTH), BF16),
        pltpu.VMEM((ts, SGU_WIDTH), BF16),
        pltpu.VMEM((ts, MEM_WIDTH), BF16),
        pltpu.VMEM((ts, MEM_WIDTH), BF16),
        pltpu.VMEM((ts, 3 * D_MODEL), F32),
    ]
    operands = (x, x, x, ktm, vm, bucket, rel_bias, sink, g_mix, wq, wkv, wz, wqm, wgl,
                sguw, sgub, lng, lnb, wba, wbs, wbm, wo)
    return pl.pallas_call(
        _mixer_kernel,
        grid=(B, nt),
        in_specs=in_specs,
        out_specs=pl.BlockSpec((1, ts, D), lambda b, i: (b, i, 0)),
        out_shape=jax.ShapeDtypeStruct((B, S, D), F32),
        scratch_shapes=scratch,
        compiler_params=pltpu.CompilerParams(
            dimension_semantics=("arbitrary", "arbitrary"), vmem_limit_bytes=VMEM_LIMIT),
        name="mixer",
    )(*operands)


def _ffn_kernel(x_ref, xp_ref, xn_ref, g_ref, wup_ref, cw_ref, cb_ref, wdn_ref, gf_ref,
                o_ref, hext_ref, act_ref, *, final_norm):
    ts = x_ref.shape[1]
    pad = BF16_ROWS
    i = pl.program_id(1)
    nt = pl.num_programs(1)
    g = g_ref[...]
    hp = _rms(xp_ref[0], g) * jnp.where(i > 0, 1.0, 0.0)
    hn = _rms(xn_ref[0], g) * jnp.where(i < nt - 1, 1.0, 0.0)
    hext_ref[0:pad] = hp.astype(BF16)
    hext_ref[pad:pad + ts] = _rms(x_ref[0], g).astype(BF16)
    hext_ref[pad + ts:] = hn.astype(BF16)

    rows_ext = ts + 2 * pad

    def up(col0):
        return _dot(hext_ref[...], wup_ref[:, pl.ds(col0, FFN_COLS)])

    def conv(a, col0):
        cols = pl.ds(col0, FFN_COLS)
        prev = pltpu.roll(a, 1, axis=0)
        nxt = pltpu.roll(a, rows_ext - 1, axis=0)
        mid = slice(pad, pad + ts)
        return (prev[mid] * cw_ref[0:1, cols] + a[mid] * cw_ref[1:2, cols]
                + nxt[mid] * cw_ref[2:3, cols] + cb_ref[:, cols])

    n_chunks = D_FF // FFN_COLS
    y = x_ref[0]
    done = 0
    for c in range(n_chunks):
        a_gate = up(c * FFN_COLS)
        a_up = up(D_FF + c * FFN_COLS)
        if c > 0 and (c % DOWN_PIECE == 0):
            k0, k1 = done * FFN_COLS, c * FFN_COLS
            y = y + _dot(act_ref[:, k0:k1], wdn_ref[k0:k1, :])
            done = c
        gate = conv(a_gate, c * FFN_COLS)
        act = (gate * conv(a_up, D_FF + c * FFN_COLS)) * _gelu_cdf(gate)
        act_ref[:, pl.ds(c * FFN_COLS, FFN_COLS)] = act.astype(BF16)
    y = y + _dot(act_ref[:, done * FFN_COLS:], wdn_ref[done * FFN_COLS:, :])
    o_ref[0] = _rms(y, gf_ref[...]) if final_norm else y


def _ffn(x, g_ffn, wup, conv_w, conv_b, wdn, g_final, final_norm):
    B, S, D = x.shape
    ts = TILE
    pad = BF16_ROWS
    assert S % ts == 0 and ts % pad == 0 and D_FF % FFN_COLS == 0
    nt = S // ts
    r = ts // pad
    nblk = S // pad
    in_specs = [
        pl.BlockSpec((1, ts, D), lambda b, i: (b, i, 0)),
        pl.BlockSpec((1, pad, D), lambda b, i: (b, jnp.maximum(i * r - 1, 0), 0)),
        pl.BlockSpec((1, pad, D), lambda b, i: (b, jnp.minimum((i + 1) * r, nblk - 1), 0)),
    ] + [_const_spec(a.shape) for a in (g_ffn, wup, conv_w, conv_b, wdn, g_final)]
    operands = (x, x, x, g_ffn, wup, conv_w, conv_b, wdn, g_final)
    return pl.pallas_call(
        functools.partial(_ffn_kernel, final_norm=final_norm),
        grid=(B, nt),
        in_specs=in_specs,
        out_specs=pl.BlockSpec((1, ts, D), lambda b, i: (b, i, 0)),
        out_shape=jax.ShapeDtypeStruct((B, S, D), F32),
        scratch_shapes=[
            pltpu.VMEM((ts + 2 * pad, D), BF16),
            pltpu.VMEM((ts, D_FF), BF16),
        ],
        compiler_params=pltpu.CompilerParams(
            dimension_semantics=("arbitrary", "arbitrary"), vmem_limit_bytes=VMEM_LIMIT),
        name="ffn",
    )(*operands)


def _t5_bucket(rel):
    nb = N_BUCKETS // 2
    max_exact = nb // 2
    ret = (rel > 0).astype(jnp.int32) * nb
    n = jnp.abs(rel)
    nf = jnp.maximum(n, 1).astype(jnp.float32)
    large = max_exact + (jnp.log(nf / max_exact) / math.log(MAX_DISTANCE / max_exact)
                         * (nb - max_exact)).astype(jnp.int32)
    large = jnp.minimum(large, nb - 1)
    return ret + jnp.where(n < max_exact, n, large)


def kernel(x_prompt, x_sample, mem_prompt, mem_sample, rel_bias, g_mix, w_in, attn_sink,
           sgu_w, sgu_b, sgu_ln_g, sgu_ln_b, g_mem, w_mem_kv, w_br_attn, w_br_sgu, w_br_mem,
           w_out, g_ffn, w_up, conv_w, conv_b, w_down, g_final):
    depth = w_in.shape[0]
    rel = (jnp.arange(3 * BLOCK)[None, :] - BLOCK) - jnp.arange(BLOCK)[:, None]
    bucket = _t5_bucket(rel)
    row = lambda a: a.reshape(1, -1)

    def encoder(x, mem):
        for l in range(depth):
            wi = w_in[l]
            wq = (wi[:, :OFF_K] * (HEAD_DIM ** -0.5)).astype(BF16)
            wkv = wi[:, OFF_K:OFF_Z].astype(BF16)
            wz = wi[:, OFF_Z:OFF_QM].astype(BF16)
            wqm = wi[:, OFF_QM:OFF_GL].astype(BF16)
            wgl = wi[:, OFF_GL:].astype(BF16)
            sgub = jnp.repeat(sgu_b[l].T, SGU_WIDTH // SGU_GROUPS, axis=1)
            ktm, vm = _mem_kv(mem, row(g_mem[l]), w_mem_kv[l].astype(BF16))
            x = _mixer(x, ktm, vm, bucket, rel_bias, attn_sink[l], row(g_mix[l]),
                       wq, wkv, wz, wqm, wgl, sgu_w[l].astype(BF16), sgub,
                       row(sgu_ln_g[l]), row(sgu_ln_b[l]),
                       w_br_attn[l].astype(BF16), w_br_sgu[l].astype(BF16),
                       w_br_mem[l].astype(BF16), w_out[l].astype(BF16))
            x = _ffn(x, row(g_ffn[l]), w_up[l].astype(BF16), conv_w[l], row(conv_b[l]),
                     w_down[l].astype(BF16), row(g_final), final_norm=(l == depth - 1))
        return x

    return encoder(x_prompt, mem_prompt), encoder(x_sample, mem_sample)
```

```python
import functools
import math

import jax
import jax.numpy as jnp
from jax import lax
from jax.experimental import pallas as pl
from jax.experimental.pallas import tpu as pltpu

D_MODEL = 1024
HEAD_DIM = 64
N_HEADS = 8
N_KV_HEADS = 2
WINDOW = 128
BLOCK = 128
N_BUCKETS = 32
MAX_DISTANCE = 128
SGU_WIDTH = 512
SGU_GROUPS = 4
SGU_CHUNK = 128
MEM_HEADS = 4
MEM_HEAD_DIM = 128
MEM_WIDTH = MEM_HEADS * MEM_HEAD_DIM
D_FF = 2816
EPS = 1e-6
NEG = -1e30

Q_W = N_HEADS * HEAD_DIM
KV_W = N_KV_HEADS * HEAD_DIM
OFF_K = Q_W
OFF_Z = Q_W + 2 * KV_W
OFF_QM = OFF_Z + 2 * SGU_WIDTH
OFF_GL = OFF_QM + MEM_WIDTH

LANES = 128
BF16_ROWS = 16
VMEM_LIMIT = 56 * 1024 * 1024

TILE = 512
FFN_COLS = 256
DOWN_PIECE = 4
PROJ_COLS = 256

BF16 = jnp.bfloat16
F32 = jnp.float32


def _rms(x, g):
    return x * lax.rsqrt(jnp.mean(x * x, axis=-1, keepdims=True) + EPS) * g


_GELU_C0 = math.sqrt(2.0 / math.pi)
_GELU_C1 = _GELU_C0 * 0.044715


def _gelu_cdf(x):
    return 0.5 + 0.5 * jnp.tanh(x * (_GELU_C0 + _GELU_C1 * (x * x)))


def _dot(a, b):
    return jnp.dot(a, b, preferred_element_type=F32)


def _const_spec(shape):
    nd = len(shape)
    return pl.BlockSpec(shape, lambda *_: (0,) * nd, pipeline_mode=pl.Buffered(1))


def _mem_kv_kernel(mem_ref, g_ref, w_ref, kt_ref, v_ref):
    h = _rms(mem_ref[0], g_ref[...]).astype(BF16)
    kv = _dot(h, w_ref[...])
    k = kv[:, :MEM_WIDTH] * (MEM_HEAD_DIM ** -0.5)
    kt_ref[0] = k.T.astype(BF16)
    v_ref[0] = kv[:, MEM_WIDTH:].astype(BF16)


def _mem_kv(mem, g_mem, w_mem_kv):
    B, M, D = mem.shape
    return pl.pallas_call(
        _mem_kv_kernel,
        grid=(B,),
        in_specs=[
            pl.BlockSpec((1, M, D), lambda b: (b, 0, 0)),
            _const_spec((1, D)),
            _const_spec((D, 2 * MEM_WIDTH)),
        ],
        out_specs=[
            pl.BlockSpec((1, MEM_WIDTH, M), lambda b: (b, 0, 0)),
            pl.BlockSpec((1, M, MEM_WIDTH), lambda b: (b, 0, 0)),
        ],
        out_shape=[
            jax.ShapeDtypeStruct((B, MEM_WIDTH, M), BF16),
            jax.ShapeDtypeStruct((B, M, MEM_WIDTH), BF16),
        ],
        compiler_params=pltpu.CompilerParams(
            dimension_semantics=("arbitrary",), vmem_limit_bytes=VMEM_LIMIT),
        name="mem_kv",
    )(mem, g_mem, w_mem_kv)


def _mixer_kernel(x_ref, xp_ref, xn_ref, ktm_ref, vm_ref, relb_ref, sink_ref,
                  g_ref, wq_ref, wkv_ref, wz_ref, wqm_ref, wgl_ref,
                  sguw_ref, sgub_ref, lng_ref, lnb_ref,
                  wba_ref, wbs_ref, wbm_ref, wo_ref,
                  o_ref,
                  bias_ref, hext_ref, kvar_ref, vvar_ref, q_ref, attn_ref,
                  u_ref, vg_ref, vln_ref, sgu_ref, qm_ref, mo_ref, gate_ref):
    ts = x_ref.shape[1]
    nq = ts // BLOCK
    i = pl.program_id(1)
    nt = pl.num_programs(1)
    first_step = jnp.logical_and(pl.program_id(0) == 0, i == 0)

    @pl.when(first_step)
    def _():
        qi = lax.broadcasted_iota(jnp.int32, (BLOCK, 3 * BLOCK), 0)
        kj = lax.broadcasted_iota(jnp.int32, (BLOCK, 3 * BLOCK), 1)
        in_window = jnp.abs(kj - BLOCK - qi) <= WINDOW
        for hd in range(N_HEADS):
            base = jnp.where(in_window, relb_ref[hd], NEG)
            kv, rem = divmod(hd, N_HEADS // N_KV_HEADS)
            half, slot = divmod(rem, 2)
            c = kv * 2 + slot
            rows = pl.ds(half * BLOCK, BLOCK)
            bias_ref[0, c, rows, :] = base
            bias_ref[1, c, rows, :] = jnp.where(kj < BLOCK, NEG, base)
            bias_ref[2, c, rows, :] = jnp.where(kj >= 2 * BLOCK, NEG, base)

    g = g_ref[...]
    hext_ref[0:BLOCK] = _rms(xp_ref[0], g).astype(BF16)
    hext_ref[BLOCK:BLOCK + ts] = _rms(x_ref[0], g).astype(BF16)
    hext_ref[BLOCK + ts:] = _rms(xn_ref[0], g).astype(BF16)
    h = hext_ref[BLOCK:BLOCK + ts]

    w = ts + 2 * BLOCK
    kv_all = _dot(hext_ref[...], wkv_ref[...])
    kt = kv_all[:, :KV_W].T.astype(BF16)
    v = kv_all[:, KV_W:]
    vswap = pltpu.roll(v, HEAD_DIM, axis=1)
    left = lax.broadcasted_iota(jnp.int32, (w, KV_W), 1) < HEAD_DIM
    zrow = jnp.zeros((HEAD_DIM, w), BF16)
    for kvh in range(N_KV_HEADS):
        kk = kt[kvh * HEAD_DIM:(kvh + 1) * HEAD_DIM]
        kvar_ref[kvh * 2 + 0] = jnp.concatenate([kk, zrow], axis=0)
        kvar_ref[kvh * 2 + 1] = jnp.concatenate([zrow, kk], axis=0)
    vvar_ref[0] = jnp.where(left, v, 0.0).astype(BF16)
    vvar_ref[1] = jnp.where(left, 0.0, vswap).astype(BF16)
    vvar_ref[2] = jnp.where(left, vswap, 0.0).astype(BF16)
    vvar_ref[3] = jnp.where(left, 0.0, v).astype(BF16)

    def proj_chunk(w_ref, col0):
        return _dot(h, w_ref[:, pl.ds(col0, PROJ_COLS)])

    def qm_task(col0):
        qm_ref[:, pl.ds(col0, PROJ_COLS)] = proj_chunk(wqm_ref, col0).astype(BF16)

    def z_task(col0):
        z = proj_chunk(wz_ref, col0)
        z = z * _gelu_cdf(z)
        if col0 < SGU_WIDTH:
            u_ref[:, pl.ds(col0, PROJ_COLS)] = z
        else:
            vg_ref[:, pl.ds(col0 - SGU_WIDTH, PROJ_COLS)] = z

    def gate_task(col0):
        gate_ref[:, pl.ds(col0, PROJ_COLS)] = jax.nn.sigmoid(proj_chunk(wgl_ref, col0))

    tasks = ([functools.partial(qm_task, c0) for c0 in range(0, MEM_WIDTH, PROJ_COLS)]
             + [functools.partial(z_task, c0) for c0 in range(0, 2 * SGU_WIDTH, PROJ_COLS)]
             + [functools.partial(gate_task, c0) for c0 in range(0, 3 * D_MODEL, PROJ_COLS)])

    def run_task():
        if tasks:
            tasks.pop(0)()

    q_ref[...] = _dot(h, wq_ref[...]).astype(BF16)
    row_half = lax.broadcasted_iota(jnp.int32, (2 * BLOCK, 1), 0) < BLOCK
    for j in range(nq):
        if j == 0:
            sel = jnp.where(i == 0, 1, 0)
        elif j == nq - 1:
            sel = jnp.where(i == nt - 1, 2, 0)
        else:
            sel = 0
        rows = pl.ds(j * BLOCK, BLOCK)
        keys = pl.ds(j * BLOCK, 3 * BLOCK)
        combos = [(kvh, slot) for kvh in range(N_KV_HEADS) for slot in range(2)]
        scores = []
        for kvh, slot in combos:
            qs = jnp.concatenate(
                [q_ref[rows, pl.ds((2 * kvh) * LANES, LANES)],
                 q_ref[rows, pl.ds((2 * kvh + 1) * LANES, LANES)]], axis=0)
            scores.append(_dot(qs, kvar_ref[kvh * 2 + slot, :, keys]))
            if (j * len(combos) + kvh * 2 + slot) % 3 != 2:
                run_task()
        probs = []
        for (kvh, slot), s in zip(combos, scores):
            s = s + bias_ref[sel, kvh * 2 + slot]
            snk = jnp.where(row_half, sink_ref[4 * kvh + slot], sink_ref[4 * kvh + 2 + slot])
            m = jnp.maximum(jnp.max(s, axis=-1, keepdims=True), snk)
            p = jnp.exp(s - m)
            den = jnp.sum(p, axis=-1, keepdims=True) + jnp.exp(snk - m)
            probs.append((p.astype(BF16), 1.0 / den))
        outs = [_dot(p, vvar_ref[kvh * 2 + slot, keys, :]) * inv
                for (kvh, slot), (p, inv) in zip(combos, probs)]
        pairs = []
        for kvh in range(N_KV_HEADS):
            acc = outs[2 * kvh] + outs[2 * kvh + 1]
            pairs += [acc[:BLOCK], acc[BLOCK:]]
        attn_ref[rows, :] = jnp.concatenate(pairs, axis=1).astype(BF16)

    head_cols = [pl.ds(hd * MEM_HEAD_DIM, MEM_HEAD_DIM) for hd in range(MEM_HEADS)]
    mem_scores = []
    for cols in head_cols:
        mem_scores.append(_dot(qm_ref[:, cols], ktm_ref[0, cols, :]))
        run_task()
    mem_probs = []
    for s in mem_scores:
        m = jnp.max(s, axis=-1, keepdims=True)
        p = jnp.exp(s - m)
        mem_probs.append((p.astype(BF16), 1.0 / jnp.sum(p, axis=-1, keepdims=True)))
    for cols, (p, inv) in zip(head_cols, mem_probs):
        mo_ref[:, cols] = (_dot(p, vm_ref[0, :, cols]) * inv).astype(BF16)

    vz = vg_ref[...]
    mu = jnp.mean(vz, axis=-1, keepdims=True)
    vc = vz - mu
    var = jnp.mean(vc * vc, axis=-1, keepdims=True)
    vln_ref[...] = (vc * lax.rsqrt(var + EPS) * lng_ref[...] + lnb_ref[...]).astype(BF16)
    for n in range(ts // SGU_CHUNK):
        rows = pl.ds(n * SGU_CHUNK, SGU_CHUNK)
        grp_cols = [pl.ds(grp * LANES, LANES) for grp in range(SGU_GROUPS)]
        svs = [_dot(sguw_ref[grp], vln_ref[rows, cols]) for grp, cols in enumerate(grp_cols)]
        for cols, sv in zip(grp_cols, svs):
            sgu_ref[rows, cols] = (u_ref[rows, cols] * (sv + sgub_ref[:, cols])).astype(BF16)
        run_task()
    while tasks:
        run_task()

    merged = None
    for br, (src_ref, w_ref) in enumerate(((attn_ref, wba_ref), (sgu_ref, wbs_ref), (mo_ref, wbm_ref))):
        term = gate_ref[:, pl.ds(br * D_MODEL, D_MODEL)] * _dot(src_ref[...], w_ref[...])
        merged = term if merged is None else merged + term
    o_ref[0] = x_ref[0] + _dot(merged.astype(BF16), wo_ref[...])


def _mixer(x, ktm, vm, rel_bias_win, sink, g_mix, wq, wkv, wz, wqm, wgl,
           sguw, sgub, lng, lnb, wba, wbs, wbm, wo):
    B, S, D = x.shape
    M = vm.shape[1]
    ts = TILE
    assert S % ts == 0 and ts % BLOCK == 0 and ts // BLOCK >= 2 and S // BLOCK >= 2
    nt = S // ts
    r = ts // BLOCK
    nblk = S // BLOCK
    w = ts + 2 * BLOCK
    smem = pl.BlockSpec(memory_space=pltpu.SMEM)
    in_specs = [
        pl.BlockSpec((1, ts, D), lambda b, i: (b, i, 0)),
        pl.BlockSpec((1, BLOCK, D), lambda b, i: (b, jnp.maximum(i * r - 1, 0), 0)),
        pl.BlockSpec((1, BLOCK, D), lambda b, i: (b, jnp.minimum((i + 1) * r, nblk - 1), 0)),
        pl.BlockSpec((1, MEM_WIDTH, M), lambda b, i: (b, 0, 0)),
        pl.BlockSpec((1, M, MEM_WIDTH), lambda b, i: (b, 0, 0)),
        _const_spec(rel_bias_win.shape), smem,
    ] + [_const_spec(a.shape) for a in (g_mix, wq, wkv, wz, wqm, wgl, sguw, sgub, lng, lnb,
                                        wba, wbs, wbm, wo)]
    scratch = [
        pltpu.VMEM((3, 4, 2 * BLOCK, 3 * BLOCK), F32),
        pltpu.VMEM((w, D), BF16),
        pltpu.VMEM((4, KV_W, w), BF16),
        pltpu.VMEM((4, w, KV_W), BF16),
        pltpu.VMEM((ts, Q_W), BF16),
        pltpu.VMEM((ts, Q_W), BF16),
        pltpu.VMEM((ts, SGU_WIDTH), F32),
        pltpu.VMEM((ts, SGU_WIDTH), F32),
        pltpu.VMEM((ts, SGU_WIDTH), BF16),
        pltpu.VMEM((ts, SGU_WIDTH), BF16),
        pltpu.VMEM((ts, MEM_WIDTH), BF16),
        pltpu.VMEM((ts, MEM_WIDTH), BF16),
        pltpu.VMEM((ts, 3 * D_MODEL), F32),
    ]
    operands = (x, x, x, ktm, vm, rel_bias_win, sink, g_mix, wq, wkv, wz, wqm, wgl,
                sguw, sgub, lng, lnb, wba, wbs, wbm, wo)
    return pl.pallas_call(
        _mixer_kernel,
        grid=(B, nt),
        in_specs=in_specs,
        out_specs=pl.BlockSpec((1, ts, D), lambda b, i: (b, i, 0)),
        out_shape=jax.ShapeDtypeStruct((B, S, D), F32),
        scratch_shapes=scratch,
        compiler_params=pltpu.CompilerParams(
            dimension_semantics=("arbitrary", "arbitrary"), vmem_limit_bytes=VMEM_LIMIT),
        name="mixer",
    )(*operands)


def _ffn_kernel(x_ref, xp_ref, xn_ref, g_ref, wup_ref, cw_ref, cb_ref, wdn_ref, gf_ref,
                o_ref, hext_ref, act_ref, *, final_norm):
    ts = x_ref.shape[1]
    pad = BF16_ROWS
    i = pl.program_id(1)
    nt = pl.num_programs(1)
    g = g_ref[...]
    hp = _rms(xp_ref[0], g) * jnp.where(i > 0, 1.0, 0.0)
    hn = _rms(xn_ref[0], g) * jnp.where(i < nt - 1, 1.0, 0.0)
    hext_ref[0:pad] = hp.astype(BF16)
    hext_ref[pad:pad + ts] = _rms(x_ref[0], g).astype(BF16)
    hext_ref[pad + ts:] = hn.astype(BF16)

    rows_ext = ts + 2 * pad

    def up(col0):
        return _dot(hext_ref[...], wup_ref[:, pl.ds(col0, FFN_COLS)])

    def conv(a, col0):
        cols = pl.ds(col0, FFN_COLS)
        prev = pltpu.roll(a, 1, axis=0)
        nxt = pltpu.roll(a, rows_ext - 1, axis=0)
        mid = slice(pad, pad + ts)
        return (prev[mid] * cw_ref[0:1, cols] + a[mid] * cw_ref[1:2, cols]
                + nxt[mid] * cw_ref[2:3, cols] + cb_ref[:, cols])

    n_chunks = D_FF // FFN_COLS
    y = x_ref[0]
    done = 0
    nxt_pair = (up(0), up(D_FF))
    for c in range(n_chunks):
        a_gate, a_up = nxt_pair
        if c + 1 < n_chunks:
            nxt_pair = (up((c + 1) * FFN_COLS), up(D_FF + (c + 1) * FFN_COLS))
        if c > 0 and (c % DOWN_PIECE == 0):
            k0, k1 = done * FFN_COLS, c * FFN_COLS
            y = y + _dot(act_ref[:, k0:k1], wdn_ref[k0:k1, :])
            done = c
        gate = conv(a_gate, c * FFN_COLS)
        act = (gate * conv(a_up, D_FF + c * FFN_COLS)) * _gelu_cdf(gate)
        act_ref[:, pl.ds(c * FFN_COLS, FFN_COLS)] = act.astype(BF16)
    y = y + _dot(act_ref[:, done * FFN_COLS:], wdn_ref[done * FFN_COLS:, :])
    o_ref[0] = _rms(y, gf_ref[...]) if final_norm else y


def _ffn(x, g_ffn, wup, conv_w, conv_b, wdn, g_final, final_norm):
    B, S, D = x.shape
    ts = TILE
    pad = BF16_ROWS
    assert S % ts == 0 and ts % pad == 0 and D_FF % FFN_COLS == 0
    nt = S // ts
    r = ts // pad
    nblk = S // pad
    in_specs = [
        pl.BlockSpec((1, ts, D), lambda b, i: (b, i, 0)),
        pl.BlockSpec((1, pad, D), lambda b, i: (b, jnp.maximum(i * r - 1, 0), 0)),
        pl.BlockSpec((1, pad, D), lambda b, i: (b, jnp.minimum((i + 1) * r, nblk - 1), 0)),
    ] + [_const_spec(a.shape) for a in (g_ffn, wup, conv_w, conv_b, wdn, g_final)]
    operands = (x, x, x, g_ffn, wup, conv_w, conv_b, wdn, g_final)
    return pl.pallas_call(
        functools.partial(_ffn_kernel, final_norm=final_norm),
        grid=(B, nt),
        in_specs=in_specs,
        out_specs=pl.BlockSpec((1, ts, D), lambda b, i: (b, i, 0)),
        out_shape=jax.ShapeDtypeStruct((B, S, D), F32),
        scratch_shapes=[
            pltpu.VMEM((ts + 2 * pad, D), BF16),
            pltpu.VMEM((ts, D_FF), BF16),
        ],
        compiler_params=pltpu.CompilerParams(
            dimension_semantics=("arbitrary", "arbitrary"), vmem_limit_bytes=VMEM_LIMIT),
        name="ffn",
    )(*operands)


def _t5_bucket(rel):
    nb = N_BUCKETS // 2
    max_exact = nb // 2
    ret = (rel > 0).astype(jnp.int32) * nb
    n = jnp.abs(rel)
    nf = jnp.maximum(n, 1).astype(jnp.float32)
    large = max_exact + (jnp.log(nf / max_exact) / math.log(MAX_DISTANCE / max_exact)
                         * (nb - max_exact)).astype(jnp.int32)
    large = jnp.minimum(large, nb - 1)
    return ret + jnp.where(n < max_exact, n, large)


def kernel(x_prompt, x_sample, mem_prompt, mem_sample, rel_bias, g_mix, w_in, attn_sink,
           sgu_w, sgu_b, sgu_ln_g, sgu_ln_b, g_mem, w_mem_kv, w_br_attn, w_br_sgu, w_br_mem,
           w_out, g_ffn, w_up, conv_w, conv_b, w_down, g_final):
    depth = w_in.shape[0]
    rel = (jnp.arange(3 * BLOCK)[None, :] - BLOCK) - jnp.arange(BLOCK)[:, None]
    rel_bias_win = rel_bias[_t5_bucket(rel)].astype(F32).transpose(2, 0, 1)
    row = lambda a: a.reshape(1, -1)

    def encoder(x, mem):
        for l in range(depth):
            wi = w_in[l]
            wq = (wi[:, :OFF_K] * (HEAD_DIM ** -0.5)).astype(BF16)
            wkv = wi[:, OFF_K:OFF_Z].astype(BF16)
            wz = wi[:, OFF_Z:OFF_QM].astype(BF16)
            wqm = wi[:, OFF_QM:OFF_GL].astype(BF16)
            wgl = wi[:, OFF_GL:].astype(BF16)
            sgub = jnp.repeat(sgu_b[l].T, SGU_WIDTH // SGU_GROUPS, axis=1)
            ktm, vm = _mem_kv(mem, row(g_mem[l]), w_mem_kv[l].astype(BF16))
            x = _mixer(x, ktm, vm, rel_bias_win, attn_sink[l], row(g_mix[l]),
                       wq, wkv, wz, wqm, wgl, sgu_w[l].astype(BF16), sgub,
                       row(sgu_ln_g[l]), row(sgu_ln_b[l]),
                       w_br_attn[l].astype(BF16), w_br_sgu[l].astype(BF16),
                       w_br_mem[l].astype(BF16), w_out[l].astype(BF16))
            x = _ffn(x, row(g_ffn[l]), w_up[l].astype(BF16), conv_w[l], row(conv_b[l]),
                     w_down[l].astype(BF16), row(g_final), final_norm=(l == depth - 1))
        return x

    return encoder(x_prompt, mem_prompt), encoder(x_sample, mem_sample)
```

```python
import functools
import math

import jax
import jax.numpy as jnp
from jax import lax
from jax.experimental import pallas as pl
from jax.experimental.pallas import tpu as pltpu

D_MODEL = 1024
HEAD_DIM = 64
N_HEADS = 8
N_KV_HEADS = 2
WINDOW = 128
BLOCK = 128
N_BUCKETS = 32
MAX_DISTANCE = 128
SGU_WIDTH = 512
SGU_GROUPS = 4
SGU_CHUNK = 128
MEM_HEADS = 4
MEM_HEAD_DIM = 128
MEM_WIDTH = MEM_HEADS * MEM_HEAD_DIM
D_FF = 2816
EPS = 1e-6
NEG = -1e30

REL_SPAN = 4 * BLOCK
Q_W = N_HEADS * HEAD_DIM
KV_W = N_KV_HEADS * HEAD_DIM
OFF_K = Q_W
OFF_Z = Q_W + 2 * KV_W
OFF_QM = OFF_Z + 2 * SGU_WIDTH
OFF_GL = OFF_QM + MEM_WIDTH

LANES = 128
BF16_ROWS = 16
VMEM_LIMIT = 56 * 1024 * 1024

TILE = 512
FFN_COLS = 256
DOWN_PIECE = 4
PROJ_COLS = 256

BF16 = jnp.bfloat16
F32 = jnp.float32


def _rms(x, g):
    return x * lax.rsqrt(jnp.mean(x * x, axis=-1, keepdims=True) + EPS) * g


_GELU_C0 = math.sqrt(2.0 / math.pi)
_GELU_C1 = _GELU_C0 * 0.044715


def _gelu_cdf(x):
    return 0.5 + 0.5 * jnp.tanh(x * (_GELU_C0 + _GELU_C1 * (x * x)))


def _dot(a, b):
    return jnp.dot(a, b, preferred_element_type=F32)


def _const_spec(shape):
    nd = len(shape)
    return pl.BlockSpec(shape, lambda *_: (0,) * nd, pipeline_mode=pl.Buffered(1))


def _mem_kv_kernel(mem_ref, g_ref, w_ref, kt_ref, v_ref):
    h = _rms(mem_ref[0], g_ref[...]).astype(BF16)
    kv = _dot(h, w_ref[...])
    k = kv[:, :MEM_WIDTH] * (MEM_HEAD_DIM ** -0.5)
    kt_ref[0] = k.T.astype(BF16)
    v_ref[0] = kv[:, MEM_WIDTH:].astype(BF16)


def _mem_kv(mem, g_mem, w_mem_kv):
    B, M, D = mem.shape
    return pl.pallas_call(
        _mem_kv_kernel,
        grid=(B,),
        in_specs=[
            pl.BlockSpec((1, M, D), lambda b: (b, 0, 0)),
            _const_spec((1, D)),
            _const_spec((D, 2 * MEM_WIDTH)),
        ],
        out_specs=[
            pl.BlockSpec((1, MEM_WIDTH, M), lambda b: (b, 0, 0)),
            pl.BlockSpec((1, M, MEM_WIDTH), lambda b: (b, 0, 0)),
        ],
        out_shape=[
            jax.ShapeDtypeStruct((B, MEM_WIDTH, M), BF16),
            jax.ShapeDtypeStruct((B, M, MEM_WIDTH), BF16),
        ],
        compiler_params=pltpu.CompilerParams(
            dimension_semantics=("arbitrary",), vmem_limit_bytes=VMEM_LIMIT),
        name="mem_kv",
    )(mem, g_mem, w_mem_kv)


def _mixer_kernel(x_ref, xp_ref, xn_ref, ktm_ref, vm_ref, relb_ref, sink_ref,
                  g_ref, wq_ref, wkv_ref, wz_ref, wqm_ref, wgl_ref,
                  sguw_ref, sgub_ref, lng_ref, lnb_ref,
                  wba_ref, wbs_ref, wbm_ref, wo_ref,
                  o_ref,
                  bias_ref, hext_ref, kvar_ref, vvar_ref, q_ref, attn_ref,
                  u_ref, vg_ref, vln_ref, sgu_ref, qm_ref, mo_ref, gate_ref):
    ts = x_ref.shape[1]
    nq = ts // BLOCK
    i = pl.program_id(1)
    nt = pl.num_programs(1)
    first_step = jnp.logical_and(pl.program_id(0) == 0, i == 0)

    @pl.when(first_step)
    def _():
        qi = lax.broadcasted_iota(jnp.int32, (BLOCK, 3 * BLOCK), 0)
        kj = lax.broadcasted_iota(jnp.int32, (BLOCK, 3 * BLOCK), 1)
        in_window = jnp.abs(kj - BLOCK - qi) <= WINDOW
        qi_wide = lax.broadcasted_iota(jnp.int32, (BLOCK, REL_SPAN), 0)
        for hd in range(N_HEADS):
            t = jnp.broadcast_to(relb_ref[hd:hd + 1, :], (BLOCK, REL_SPAN))
            for bit in range(BLOCK.bit_length() - 1):
                t = jnp.where((qi_wide >> bit) & 1 == 1, pltpu.roll(t, 1 << bit, axis=1), t)
            base = jnp.where(in_window, t[:, :3 * BLOCK], NEG)
            kv, rem = divmod(hd, N_HEADS // N_KV_HEADS)
            half, slot = divmod(rem, 2)
            c = kv * 2 + slot
            rows = pl.ds(half * BLOCK, BLOCK)
            bias_ref[0, c, rows, :] = base
            bias_ref[1, c, rows, :] = jnp.where(kj < BLOCK, NEG, base)
            bias_ref[2, c, rows, :] = jnp.where(kj >= 2 * BLOCK, NEG, base)

    g = g_ref[...]
    hext_ref[0:BLOCK] = _rms(xp_ref[0], g).astype(BF16)
    hext_ref[BLOCK:BLOCK + ts] = _rms(x_ref[0], g).astype(BF16)
    hext_ref[BLOCK + ts:] = _rms(xn_ref[0], g).astype(BF16)
    h = hext_ref[BLOCK:BLOCK + ts]

    w = ts + 2 * BLOCK
    kv_all = _dot(hext_ref[...], wkv_ref[...])
    kt = kv_all[:, :KV_W].T.astype(BF16)
    v = kv_all[:, KV_W:]
    vswap = pltpu.roll(v, HEAD_DIM, axis=1)
    left = lax.broadcasted_iota(jnp.int32, (w, KV_W), 1) < HEAD_DIM
    zrow = jnp.zeros((HEAD_DIM, w), BF16)
    for kvh in range(N_KV_HEADS):
        kk = kt[kvh * HEAD_DIM:(kvh + 1) * HEAD_DIM]
        kvar_ref[kvh * 2 + 0] = jnp.concatenate([kk, zrow], axis=0)
        kvar_ref[kvh * 2 + 1] = jnp.concatenate([zrow, kk], axis=0)
    vvar_ref[0] = jnp.where(left, v, 0.0).astype(BF16)
    vvar_ref[1] = jnp.where(left, 0.0, vswap).astype(BF16)
    vvar_ref[2] = jnp.where(left, vswap, 0.0).astype(BF16)
    vvar_ref[3] = jnp.where(left, 0.0, v).astype(BF16)

    def proj_chunk(w_ref, col0):
        return _dot(h, w_ref[:, pl.ds(col0, PROJ_COLS)])

    def qm_task(col0):
        qm_ref[:, pl.ds(col0, PROJ_COLS)] = proj_chunk(wqm_ref, col0).astype(BF16)

    def z_task(col0):
        z = proj_chunk(wz_ref, col0)
        z = z * _gelu_cdf(z)
        if col0 < SGU_WIDTH:
            u_ref[:, pl.ds(col0, PROJ_COLS)] = z
        else:
            vg_ref[:, pl.ds(col0 - SGU_WIDTH, PROJ_COLS)] = z

    def gate_task(col0):
        gate_ref[:, pl.ds(col0, PROJ_COLS)] = jax.nn.sigmoid(proj_chunk(wgl_ref, col0))

    tasks = ([functools.partial(qm_task, c0) for c0 in range(0, MEM_WIDTH, PROJ_COLS)]
             + [functools.partial(z_task, c0) for c0 in range(0, 2 * SGU_WIDTH, PROJ_COLS)]
             + [functools.partial(gate_task, c0) for c0 in range(0, 3 * D_MODEL, PROJ_COLS)])

    def run_task():
        if tasks:
            tasks.pop(0)()

    q_ref[...] = _dot(h, wq_ref[...]).astype(BF16)
    row_half = lax.broadcasted_iota(jnp.int32, (2 * BLOCK, 1), 0) < BLOCK
    for j in range(nq):
        if j == 0:
            sel = jnp.where(i == 0, 1, 0)
        elif j == nq - 1:
            sel = jnp.where(i == nt - 1, 2, 0)
        else:
            sel = 0
        rows = pl.ds(j * BLOCK, BLOCK)
        keys = pl.ds(j * BLOCK, 3 * BLOCK)
        combos = [(kvh, slot) for kvh in range(N_KV_HEADS) for slot in range(2)]
        scores = []
        for kvh, slot in combos:
            qs = jnp.concatenate(
                [q_ref[rows, pl.ds((2 * kvh) * LANES, LANES)],
                 q_ref[rows, pl.ds((2 * kvh + 1) * LANES, LANES)]], axis=0)
            scores.append(_dot(qs, kvar_ref[kvh * 2 + slot, :, keys]))
            if (j * len(combos) + kvh * 2 + slot) % 3 != 2:
                run_task()
        probs = []
        for (kvh, slot), s in zip(combos, scores):
            s = s + bias_ref[sel, kvh * 2 + slot]
            snk = jnp.where(row_half, sink_ref[4 * kvh + slot], sink_ref[4 * kvh + 2 + slot])
            m = jnp.maximum(jnp.max(s, axis=-1, keepdims=True), snk)
            p = jnp.exp(s - m)
            den = jnp.sum(p, axis=-1, keepdims=True) + jnp.exp(snk - m)
            probs.append((p.astype(BF16), 1.0 / den))
        outs = [_dot(p, vvar_ref[kvh * 2 + slot, keys, :]) * inv
                for (kvh, slot), (p, inv) in zip(combos, probs)]
        pairs = []
        for kvh in range(N_KV_HEADS):
            acc = outs[2 * kvh] + outs[2 * kvh + 1]
            pairs += [acc[:BLOCK], acc[BLOCK:]]
        attn_ref[rows, :] = jnp.concatenate(pairs, axis=1).astype(BF16)

    head_cols = [pl.ds(hd * MEM_HEAD_DIM, MEM_HEAD_DIM) for hd in range(MEM_HEADS)]
    mem_scores = []
    for cols in head_cols:
        mem_scores.append(_dot(qm_ref[:, cols], ktm_ref[0, cols, :]))
        run_task()
    mem_probs = []
    for s in mem_scores:
        m = jnp.max(s, axis=-1, keepdims=True)
        p = jnp.exp(s - m)
        mem_probs.append((p.astype(BF16), 1.0 / jnp.sum(p, axis=-1, keepdims=True)))
    for cols, (p, inv) in zip(head_cols, mem_probs):
        mo_ref[:, cols] = (_dot(p, vm_ref[0, :, cols]) * inv).astype(BF16)

    vz = vg_ref[...]
    mu = jnp.mean(vz, axis=-1, keepdims=True)
    vc = vz - mu
    var = jnp.mean(vc * vc, axis=-1, keepdims=True)
    vln_ref[...] = (vc * lax.rsqrt(var + EPS) * lng_ref[...] + lnb_ref[...]).astype(BF16)
    for n in range(ts // SGU_CHUNK):
        rows = pl.ds(n * SGU_CHUNK, SGU_CHUNK)
        grp_cols = [pl.ds(grp * LANES, LANES) for grp in range(SGU_GROUPS)]
        svs = [_dot(sguw_ref[grp], vln_ref[rows, cols]) for grp, cols in enumerate(grp_cols)]
        for cols, sv in zip(grp_cols, svs):
            sgu_ref[rows, cols] = (u_ref[rows, cols] * (sv + sgub_ref[:, cols])).astype(BF16)
        run_task()
    while tasks:
        run_task()

    merged = None
    for br, (src_ref, w_ref) in enumerate(((attn_ref, wba_ref), (sgu_ref, wbs_ref), (mo_ref, wbm_ref))):
        term = gate_ref[:, pl.ds(br * D_MODEL, D_MODEL)] * _dot(src_ref[...], w_ref[...])
        merged = term if merged is None else merged + term
    o_ref[0] = x_ref[0] + _dot(merged.astype(BF16), wo_ref[...])


def _mixer(x, ktm, vm, rel_bias_win, sink, g_mix, wq, wkv, wz, wqm, wgl,
           sguw, sgub, lng, lnb, wba, wbs, wbm, wo):
    B, S, D = x.shape
    M = vm.shape[1]
    ts = TILE
    assert S % ts == 0 and ts % BLOCK == 0 and ts // BLOCK >= 2 and S // BLOCK >= 2
    nt = S // ts
    r = ts // BLOCK
    nblk = S // BLOCK
    w = ts + 2 * BLOCK
    smem = pl.BlockSpec(memory_space=pltpu.SMEM)
    in_specs = [
        pl.BlockSpec((1, ts, D), lambda b, i: (b, i, 0)),
        pl.BlockSpec((1, BLOCK, D), lambda b, i: (b, jnp.maximum(i * r - 1, 0), 0)),
        pl.BlockSpec((1, BLOCK, D), lambda b, i: (b, jnp.minimum((i + 1) * r, nblk - 1), 0)),
        pl.BlockSpec((1, MEM_WIDTH, M), lambda b, i: (b, 0, 0)),
        pl.BlockSpec((1, M, MEM_WIDTH), lambda b, i: (b, 0, 0)),
        _const_spec(rel_bias_win.shape), smem,
    ] + [_const_spec(a.shape) for a in (g_mix, wq, wkv, wz, wqm, wgl, sguw, sgub, lng, lnb,
                                        wba, wbs, wbm, wo)]
    scratch = [
        pltpu.VMEM((3, 4, 2 * BLOCK, 3 * BLOCK), F32),
        pltpu.VMEM((w, D), BF16),
        pltpu.VMEM((4, KV_W, w), BF16),
        pltpu.VMEM((4, w, KV_W), BF16),
        pltpu.VMEM((ts, Q_W), BF16),
        pltpu.VMEM((ts, Q_W), BF16),
        pltpu.VMEM((ts, SGU_WIDTH), F32),
        pltpu.VMEM((ts, SGU_WIDTH), F32),
        pltpu.VMEM((ts, SGU_WIDTH), BF16),
        pltpu.VMEM((ts, SGU_WIDTH), BF16),
        pltpu.VMEM((ts, MEM_WIDTH), BF16),
        pltpu.VMEM((ts, MEM_WIDTH), BF16),
        pltpu.VMEM((ts, 3 * D_MODEL), F32),
    ]
    operands = (x, x, x, ktm, vm, rel_bias_win, sink, g_mix, wq, wkv, wz, wqm, wgl,
                sguw, sgub, lng, lnb, wba, wbs, wbm, wo)
    return pl.pallas_call(
        _mixer_kernel,
        grid=(B, nt),
        in_specs=in_specs,
        out_specs=pl.BlockSpec((1, ts, D), lambda b, i: (b, i, 0)),
        out_shape=jax.ShapeDtypeStruct((B, S, D), F32),
        scratch_shapes=scratch,
        compiler_params=pltpu.CompilerParams(
            dimension_semantics=("arbitrary", "arbitrary"), vmem_limit_bytes=VMEM_LIMIT),
        name="mixer",
    )(*operands)


def _ffn_kernel(x_ref, xp_ref, xn_ref, g_ref, wup_ref, cw_ref, cb_ref, wdn_ref, gf_ref,
                o_ref, hext_ref, act_ref, *, final_norm):
    ts = x_ref.shape[1]
    pad = BF16_ROWS
    i = pl.program_id(1)
    nt = pl.num_programs(1)
    g = g_ref[...]
    hp = _rms(xp_ref[0], g) * jnp.where(i > 0, 1.0, 0.0)
    hn = _rms(xn_ref[0], g) * jnp.where(i < nt - 1, 1.0, 0.0)
    hext_ref[0:pad] = hp.astype(BF16)
    hext_ref[pad:pad + ts] = _rms(x_ref[0], g).astype(BF16)
    hext_ref[pad + ts:] = hn.astype(BF16)

    rows_ext = ts + 2 * pad

    def up(col0):
        return _dot(hext_ref[...], wup_ref[:, pl.ds(col0, FFN_COLS)])

    def conv(a, col0):
        cols = pl.ds(col0, FFN_COLS)
        prev = pltpu.roll(a, 1, axis=0)
        nxt = pltpu.roll(a, rows_ext - 1, axis=0)
        mid = slice(pad, pad + ts)
        return (prev[mid] * cw_ref[0:1, cols] + a[mid] * cw_ref[1:2, cols]
                + nxt[mid] * cw_ref[2:3, cols] + cb_ref[:, cols])

    n_chunks = D_FF // FFN_COLS
    y = x_ref[0]
    done = 0
    nxt_pair = (up(0), up(D_FF))
    for c in range(n_chunks):
        a_gate, a_up = nxt_pair
        if c + 1 < n_chunks:
            nxt_pair = (up((c + 1) * FFN_COLS), up(D_FF + (c + 1) * FFN_COLS))
        if c > 0 and (c % DOWN_PIECE == 0):
            k0, k1 = done * FFN_COLS, c * FFN_COLS
            y = y + _dot(act_ref[:, k0:k1], wdn_ref[k0:k1, :])
            done = c
        gate = conv(a_gate, c * FFN_COLS)
        act = (gate * conv(a_up, D_FF + c * FFN_COLS)) * _gelu_cdf(gate)
        act_ref[:, pl.ds(c * FFN_COLS, FFN_COLS)] = act.astype(BF16)
    y = y + _dot(act_ref[:, done * FFN_COLS:], wdn_ref[done * FFN_COLS:, :])
    o_ref[0] = _rms(y, gf_ref[...]) if final_norm else y


def _ffn(x, g_ffn, wup, conv_w, conv_b, wdn, g_final, final_norm):
    B, S, D = x.shape
    ts = TILE
    pad = BF16_ROWS
    assert S % ts == 0 and ts % pad == 0 and D_FF % FFN_COLS == 0
    nt = S // ts
    r = ts // pad
    nblk = S // pad
    in_specs = [
        pl.BlockSpec((1, ts, D), lambda b, i: (b, i, 0)),
        pl.BlockSpec((1, pad, D), lambda b, i: (b, jnp.maximum(i * r - 1, 0), 0)),
        pl.BlockSpec((1, pad, D), lambda b, i: (b, jnp.minimum((i + 1) * r, nblk - 1), 0)),
    ] + [_const_spec(a.shape) for a in (g_ffn, wup, conv_w, conv_b, wdn, g_final)]
    operands = (x, x, x, g_ffn, wup, conv_w, conv_b, wdn, g_final)
    return pl.pallas_call(
        functools.partial(_ffn_kernel, final_norm=final_norm),
        grid=(B, nt),
        in_specs=in_specs,
        out_specs=pl.BlockSpec((1, ts, D), lambda b, i: (b, i, 0)),
        out_shape=jax.ShapeDtypeStruct((B, S, D), F32),
        scratch_shapes=[
            pltpu.VMEM((ts + 2 * pad, D), BF16),
            pltpu.VMEM((ts, D_FF), BF16),
        ],
        compiler_params=pltpu.CompilerParams(
            dimension_semantics=("arbitrary", "arbitrary"), vmem_limit_bytes=VMEM_LIMIT),
        name="ffn",
    )(*operands)


def _t5_bucket(rel):
    nb = N_BUCKETS // 2
    max_exact = nb // 2
    ret = (rel > 0).astype(jnp.int32) * nb
    n = jnp.abs(rel)
    nf = jnp.maximum(n, 1).astype(jnp.float32)
    large = max_exact + (jnp.log(nf / max_exact) / math.log(MAX_DISTANCE / max_exact)
                         * (nb - max_exact)).astype(jnp.int32)
    large = jnp.minimum(large, nb - 1)
    return ret + jnp.where(n < max_exact, n, large)


def kernel(x_prompt, x_sample, mem_prompt, mem_sample, rel_bias, g_mix, w_in, attn_sink,
           sgu_w, sgu_b, sgu_ln_g, sgu_ln_b, g_mem, w_mem_kv, w_br_attn, w_br_sgu, w_br_mem,
           w_out, g_ffn, w_up, conv_w, conv_b, w_down, g_final):
    depth = w_in.shape[0]
    d = jnp.arange(REL_SPAN)
    rel = jnp.where(d < 3 * BLOCK, d, d - REL_SPAN) - BLOCK
    rel_bias_win = rel_bias[_t5_bucket(rel)].astype(F32).T
    row = lambda a: a.reshape(1, -1)

    def encoder(x, mem):
        for l in range(depth):
            wi = w_in[l]
            wq = (wi[:, :OFF_K] * (HEAD_DIM ** -0.5)).astype(BF16)
            wkv = wi[:, OFF_K:OFF_Z].astype(BF16)
            wz = wi[:, OFF_Z:OFF_QM].astype(BF16)
            wqm = wi[:, OFF_QM:OFF_GL].astype(BF16)
            wgl = wi[:, OFF_GL:].astype(BF16)
            sgub = jnp.repeat(sgu_b[l].T, SGU_WIDTH // SGU_GROUPS, axis=1)
            ktm, vm = _mem_kv(mem, row(g_mem[l]), w_mem_kv[l].astype(BF16))
            x = _mixer(x, ktm, vm, rel_bias_win, attn_sink[l], row(g_mix[l]),
                       wq, wkv, wz, wqm, wgl, sgu_w[l].astype(BF16), sgub,
                       row(sgu_ln_g[l]), row(sgu_ln_b[l]),
                       w_br_attn[l].astype(BF16), w_br_sgu[l].astype(BF16),
                       w_br_mem[l].astype(BF16), w_out[l].astype(BF16))
            x = _ffn(x, row(g_ffn[l]), w_up[l].astype(BF16), conv_w[l], row(conv_b[l]),
                     w_down[l].astype(BF16), row(g_final), final_norm=(l == depth - 1))
        return x

    return encoder(x_prompt, mem_prompt), encoder(x_sample, mem_sample)
```

```python
import functools
import math

import jax
import jax.numpy as jnp
from jax import lax
from jax.experimental import pallas as pl
from jax.experimental.pallas import tpu as pltpu

D_MODEL = 1024
HEAD_DIM = 64
N_HEADS = 8
N_KV_HEADS = 2
WINDOW = 128
BLOCK = 128
N_BUCKETS = 32
MAX_DISTANCE = 128
SGU_WIDTH = 512
SGU_GROUPS = 4
SGU_CHUNK = 128
MEM_HEADS = 4
MEM_HEAD_DIM = 128
MEM_WIDTH = MEM_HEADS * MEM_HEAD_DIM
D_FF = 2816
EPS = 1e-6
NEG = -1e30

REL_SPAN = 4 * BLOCK
Q_W = N_HEADS * HEAD_DIM
KV_W = N_KV_HEADS * HEAD_DIM
OFF_K = Q_W
OFF_Z = Q_W + 2 * KV_W
OFF_QM = OFF_Z + 2 * SGU_WIDTH
OFF_GL = OFF_QM + MEM_WIDTH

LANES = 128
SUBLANES = 8
BF16_ROWS = 16
VMEM_LIMIT = 56 * 1024 * 1024

TILE = 512
FFN_COLS = 256
DOWN_FLUSH = (4, 8, 10)
ATTN_GROUP = 1
PROJ_COLS = 256

BF16 = jnp.bfloat16
F32 = jnp.float32


def _rms(x, g):
    return x * lax.rsqrt(jnp.mean(x * x, axis=-1, keepdims=True) + EPS) * g


_GELU_C0 = math.sqrt(2.0 / math.pi)
_GELU_C1 = _GELU_C0 * 0.044715


def _gelu_cdf(x):
    return 0.5 + 0.5 * jnp.tanh(x * (_GELU_C0 + _GELU_C1 * (x * x)))


def _dot(a, b):
    return jnp.dot(a, b, preferred_element_type=F32)


def _const_spec(shape):
    nd = len(shape)
    return pl.BlockSpec(shape, lambda *_: (0,) * nd, pipeline_mode=pl.Buffered(1))


def _mem_kv_kernel(mem_ref, g_ref, w_ref, kt_ref, v_ref):
    h = _rms(mem_ref[0], g_ref[...]).astype(BF16)
    kv = _dot(h, w_ref[...])
    k = kv[:, :MEM_WIDTH] * (MEM_HEAD_DIM ** -0.5)
    kt_ref[0] = k.T.astype(BF16)
    v_ref[0] = kv[:, MEM_WIDTH:].astype(BF16)


def _mem_kv(mem, g_mem, w_mem_kv):
    B, M, D = mem.shape
    return pl.pallas_call(
        _mem_kv_kernel,
        grid=(B,),
        in_specs=[
            pl.BlockSpec((1, M, D), lambda b: (b, 0, 0)),
            _const_spec((1, D)),
            _const_spec((D, 2 * MEM_WIDTH)),
        ],
        out_specs=[
            pl.BlockSpec((1, MEM_WIDTH, M), lambda b: (b, 0, 0)),
            pl.BlockSpec((1, M, MEM_WIDTH), lambda b: (b, 0, 0)),
        ],
        out_shape=[
            jax.ShapeDtypeStruct((B, MEM_WIDTH, M), BF16),
            jax.ShapeDtypeStruct((B, M, MEM_WIDTH), BF16),
        ],
        compiler_params=pltpu.CompilerParams(
            dimension_semantics=("arbitrary",), vmem_limit_bytes=VMEM_LIMIT),
        name="mem_kv",
    )(mem, g_mem, w_mem_kv)


def _mixer_kernel(x_ref, xp_ref, xn_ref, ktm_ref, vm_ref, relb_ref, sink_ref,
                  g_ref, win_ref,
                  sguw_ref, sgub_ref, lng_ref, lnb_ref,
                  wba_ref, wbs_ref, wbm_ref, wo_ref,
                  o_ref,
                  bias_ref, hext_ref, kvar_ref, vvar_ref, q_ref, attn_ref,
                  u_ref, vg_ref, vln_ref, sgu_ref, qm_ref, mo_ref, gate_ref):
    ts = x_ref.shape[1]
    nq = ts // BLOCK
    i = pl.program_id(1)
    nt = pl.num_programs(1)
    first_step = jnp.logical_and(pl.program_id(0) == 0, i == 0)

    @pl.when(first_step)
    def _():
        qi = lax.broadcasted_iota(jnp.int32, (BLOCK, 3 * BLOCK), 0)
        kj = lax.broadcasted_iota(jnp.int32, (BLOCK, 3 * BLOCK), 1)
        in_window = jnp.abs(kj - BLOCK - qi) <= WINDOW
        sub = lax.broadcasted_iota(jnp.int32, (SUBLANES, REL_SPAN), 0)
        for hd in range(N_HEADS):
            t8 = jnp.broadcast_to(relb_ref[hd:hd + 1, :], (SUBLANES, REL_SPAN))
            for bit in range(SUBLANES.bit_length() - 1):
                t8 = jnp.where((sub >> bit) & 1 == 1, pltpu.roll(t8, 1 << bit, axis=1), t8)
            t = jnp.concatenate(
                [t8] + [pltpu.roll(t8, SUBLANES * k, axis=1) for k in range(1, BLOCK // SUBLANES)],
                axis=0)
            base = jnp.where(in_window, t[:, :3 * BLOCK], NEG)
            kv, rem = divmod(hd, N_HEADS // N_KV_HEADS)
            half, slot = divmod(rem, 2)
            c = kv * 2 + slot
            rows = pl.ds(half * BLOCK, BLOCK)
            bias_ref[0, c, rows, :] = base
            bias_ref[1, c, rows, :] = jnp.where(kj < BLOCK, NEG, base)
            bias_ref[2, c, rows, :] = jnp.where(kj >= 2 * BLOCK, NEG, base)

    g = g_ref[...]
    hext_ref[0:BLOCK] = _rms(xp_ref[0], g).astype(BF16)
    hext_ref[BLOCK:BLOCK + ts] = _rms(x_ref[0], g).astype(BF16)
    hext_ref[BLOCK + ts:] = _rms(xn_ref[0], g).astype(BF16)
    h = hext_ref[BLOCK:BLOCK + ts]

    w = ts + 2 * BLOCK
    kv_all = _dot(hext_ref[...], win_ref[:, pl.ds(OFF_K, 2 * KV_W)])
    kt = kv_all[:, :KV_W].T.astype(BF16)
    v = kv_all[:, KV_W:]
    vswap = pltpu.roll(v, HEAD_DIM, axis=1)
    left = lax.broadcasted_iota(jnp.int32, (w, KV_W), 1) < HEAD_DIM
    zrow = jnp.zeros((HEAD_DIM, w), BF16)
    for kvh in range(N_KV_HEADS):
        kk = kt[kvh * HEAD_DIM:(kvh + 1) * HEAD_DIM]
        kvar_ref[kvh * 2 + 0] = jnp.concatenate([kk, zrow], axis=0)
        kvar_ref[kvh * 2 + 1] = jnp.concatenate([zrow, kk], axis=0)
    vvar_ref[0] = jnp.where(left, v, 0.0).astype(BF16)
    vvar_ref[1] = jnp.where(left, 0.0, vswap).astype(BF16)
    vvar_ref[2] = jnp.where(left, vswap, 0.0).astype(BF16)
    vvar_ref[3] = jnp.where(left, 0.0, v).astype(BF16)

    def proj_chunk(off, col0):
        return _dot(h, win_ref[:, pl.ds(off + col0, PROJ_COLS)])

    def qm_task(col0):
        qm_ref[:, pl.ds(col0, PROJ_COLS)] = proj_chunk(OFF_QM, col0).astype(BF16)

    def z_task(col0):
        z = proj_chunk(OFF_Z, col0)
        z = z * _gelu_cdf(z)
        if col0 < SGU_WIDTH:
            u_ref[:, pl.ds(col0, PROJ_COLS)] = z
        else:
            vg_ref[:, pl.ds(col0 - SGU_WIDTH, PROJ_COLS)] = z

    def gate_task(col0):
        gate_ref[:, pl.ds(col0, PROJ_COLS)] = jax.nn.sigmoid(proj_chunk(OFF_GL, col0))

    tasks = ([functools.partial(qm_task, c0) for c0 in range(0, MEM_WIDTH, PROJ_COLS)]
             + [functools.partial(z_task, c0) for c0 in range(0, 2 * SGU_WIDTH, PROJ_COLS)]
             + [functools.partial(gate_task, c0) for c0 in range(0, 3 * D_MODEL, PROJ_COLS)])

    def run_task():
        if tasks:
            tasks.pop(0)()

    q_ref[...] = _dot(h, win_ref[:, pl.ds(0, Q_W)]).astype(BF16)
    row_half = lax.broadcasted_iota(jnp.int32, (2 * BLOCK, 1), 0) < BLOCK
    def block_sel(j):
        if j == 0:
            return jnp.where(i == 0, 1, 0)
        if j == nq - 1:
            return jnp.where(i == nt - 1, 2, 0)
        return 0

    slot_no = 0
    for j0 in range(0, nq, ATTN_GROUP):
        combos = [(j, kvh, slot) for j in range(j0, j0 + ATTN_GROUP)
                  for kvh in range(N_KV_HEADS) for slot in range(2)]
        scores = []
        for j, kvh, slot in combos:
            rows = pl.ds(j * BLOCK, BLOCK)
            qs = jnp.concatenate(
                [q_ref[rows, pl.ds((2 * kvh) * LANES, LANES)],
                 q_ref[rows, pl.ds((2 * kvh + 1) * LANES, LANES)]], axis=0)
            scores.append(_dot(qs, kvar_ref[kvh * 2 + slot, :, pl.ds(j * BLOCK, 3 * BLOCK)]))
            if slot_no % 3 != 2:
                run_task()
            slot_no += 1
        probs = []
        for (j, kvh, slot), s in zip(combos, scores):
            s = s + bias_ref[block_sel(j), kvh * 2 + slot]
            snk = jnp.where(row_half, sink_ref[4 * kvh + slot], sink_ref[4 * kvh + 2 + slot])
            m = jnp.maximum(jnp.max(s, axis=-1, keepdims=True), snk)
            p = jnp.exp(s - m)
            den = jnp.sum(p, axis=-1, keepdims=True) + jnp.exp(snk - m)
            probs.append((p.astype(BF16), 1.0 / den))
        outs = [_dot(p, vvar_ref[kvh * 2 + slot, pl.ds(j * BLOCK, 3 * BLOCK), :]) * inv
                for (j, kvh, slot), (p, inv) in zip(combos, probs)]
        for g in range(ATTN_GROUP):
            pairs = []
            for kvh in range(N_KV_HEADS):
                base = (g * N_KV_HEADS + kvh) * 2
                acc = outs[base] + outs[base + 1]
                pairs += [acc[:BLOCK], acc[BLOCK:]]
            attn_ref[pl.ds((j0 + g) * BLOCK, BLOCK), :] = jnp.concatenate(pairs, axis=1).astype(BF16)

    head_cols = [pl.ds(hd * MEM_HEAD_DIM, MEM_HEAD_DIM) for hd in range(MEM_HEADS)]
    mem_scores = []
    for cols in head_cols:
        mem_scores.append(_dot(qm_ref[:, cols], ktm_ref[0, cols, :]))
        run_task()
    mem_probs = []
    for s in mem_scores:
        m = jnp.max(s, axis=-1, keepdims=True)
        p = jnp.exp(s - m)
        mem_probs.append((p.astype(BF16), 1.0 / jnp.sum(p, axis=-1, keepdims=True)))
    for cols, (p, inv) in zip(head_cols, mem_probs):
        mo_ref[:, cols] = (_dot(p, vm_ref[0, :, cols]) * inv).astype(BF16)

    vz = vg_ref[...]
    mu = jnp.mean(vz, axis=-1, keepdims=True)
    vc = vz - mu
    var = jnp.mean(vc * vc, axis=-1, keepdims=True)
    vln_ref[...] = (vc * lax.rsqrt(var + EPS) * lng_ref[...] + lnb_ref[...]).astype(BF16)
    for n in range(ts // SGU_CHUNK):
        rows = pl.ds(n * SGU_CHUNK, SGU_CHUNK)
        grp_cols = [pl.ds(grp * LANES, LANES) for grp in range(SGU_GROUPS)]
        svs = [_dot(sguw_ref[grp], vln_ref[rows, cols]) for grp, cols in enumerate(grp_cols)]
        for cols, sv in zip(grp_cols, svs):
            sgu_ref[rows, cols] = (u_ref[rows, cols] * (sv + sgub_ref[:, cols])).astype(BF16)
        run_task()
    while tasks:
        run_task()

    merged = None
    for br, (src_ref, w_ref) in enumerate(((attn_ref, wba_ref), (sgu_ref, wbs_ref), (mo_ref, wbm_ref))):
        term = gate_ref[:, pl.ds(br * D_MODEL, D_MODEL)] * _dot(src_ref[...], w_ref[...])
        merged = term if merged is None else merged + term
    o_ref[0] = x_ref[0] + _dot(merged.astype(BF16), wo_ref[...])


def _mixer(x, ktm, vm, rel_bias_win, sink, g_mix, win,
           sguw, sgub, lng, lnb, wba, wbs, wbm, wo):
    B, S, D = x.shape
    M = vm.shape[1]
    ts = TILE
    assert S % ts == 0 and ts % BLOCK == 0 and ts // BLOCK >= 2 and S // BLOCK >= 2
    nt = S // ts
    r = ts // BLOCK
    nblk = S // BLOCK
    w = ts + 2 * BLOCK
    smem = pl.BlockSpec(memory_space=pltpu.SMEM)
    in_specs = [
        pl.BlockSpec((1, ts, D), lambda b, i: (b, i, 0)),
        pl.BlockSpec((1, BLOCK, D), lambda b, i: (b, jnp.maximum(i * r - 1, 0), 0)),
        pl.BlockSpec((1, BLOCK, D), lambda b, i: (b, jnp.minimum((i + 1) * r, nblk - 1), 0)),
        pl.BlockSpec((1, MEM_WIDTH, M), lambda b, i: (b, 0, 0)),
        pl.BlockSpec((1, M, MEM_WIDTH), lambda b, i: (b, 0, 0)),
        _const_spec(rel_bias_win.shape), smem,
    ] + [_const_spec(a.shape) for a in (g_mix, win, sguw, sgub, lng, lnb,
                                        wba, wbs, wbm, wo)]
    scratch = [
        pltpu.VMEM((3, 4, 2 * BLOCK, 3 * BLOCK), F32),
        pltpu.VMEM((w, D), BF16),
        pltpu.VMEM((4, KV_W, w), BF16),
        pltpu.VMEM((4, w, KV_W), BF16),
        pltpu.VMEM((ts, Q_W), BF16),
        pltpu.VMEM((ts, Q_W), BF16),
        pltpu.VMEM((ts, SGU_WIDTH), F32),
        pltpu.VMEM((ts, SGU_WIDTH), F32),
        pltpu.VMEM((ts, SGU_WIDTH), BF16),
        pltpu.VMEM((ts, SGU_WIDTH), BF16),
        pltpu.VMEM((ts, MEM_WIDTH), BF16),
        pltpu.VMEM((ts, MEM_WIDTH), BF16),
        pltpu.VMEM((ts, 3 * D_MODEL), F32),
    ]
    operands = (x, x, x, ktm, vm, rel_bias_win, sink, g_mix, win,
                sguw, sgub, lng, lnb, wba, wbs, wbm, wo)
    return pl.pallas_call(
        _mixer_kernel,
        grid=(B, nt),
        in_specs=in_specs,
        out_specs=pl.BlockSpec((1, ts, D), lambda b, i: (b, i, 0)),
        out_shape=jax.ShapeDtypeStruct((B, S, D), F32),
        scratch_shapes=scratch,
        compiler_params=pltpu.CompilerParams(
            dimension_semantics=("arbitrary", "arbitrary"), vmem_limit_bytes=VMEM_LIMIT),
        name="mixer",
    )(*operands)


def _ffn_kernel(x_ref, xp_ref, xn_ref, g_ref, wup_ref, cw_ref, cb_ref, wdn_ref, gf_ref,
                o_ref, hext_ref, act_ref, *, final_norm):
    ts = x_ref.shape[1]
    pad = BF16_ROWS
    i = pl.program_id(1)
    nt = pl.num_programs(1)
    g = g_ref[...]
    hp = _rms(xp_ref[0], g) * jnp.where(i > 0, 1.0, 0.0)
    hn = _rms(xn_ref[0], g) * jnp.where(i < nt - 1, 1.0, 0.0)
    hext_ref[0:pad] = hp.astype(BF16)
    hext_ref[pad:pad + ts] = _rms(x_ref[0], g).astype(BF16)
    hext_ref[pad + ts:] = hn.astype(BF16)

    rows_ext = ts + 2 * pad

    def up(col0):
        return _dot(hext_ref[...], wup_ref[:, pl.ds(col0, FFN_COLS)])

    def conv(a, col0):
        cols = pl.ds(col0, FFN_COLS)
        prev = pltpu.roll(a, 1, axis=0)
        nxt = pltpu.roll(a, rows_ext - 1, axis=0)
        mid = slice(pad, pad + ts)
        return (prev[mid] * cw_ref[0:1, cols] + a[mid] * cw_ref[1:2, cols]
                + nxt[mid] * cw_ref[2:3, cols] + cb_ref[:, cols])

    n_chunks = D_FF // FFN_COLS
    y = x_ref[0]
    done = 0
    nxt_pair = (up(0), up(D_FF))
    for c in range(n_chunks):
        a_gate, a_up = nxt_pair
        if c + 1 < n_chunks:
            nxt_pair = (up((c + 1) * FFN_COLS), up(D_FF + (c + 1) * FFN_COLS))
        if c in DOWN_FLUSH:
            k0, k1 = done * FFN_COLS, c * FFN_COLS
            y = y + _dot(act_ref[:, k0:k1], wdn_ref[k0:k1, :])
            done = c
        gate = conv(a_gate, c * FFN_COLS)
        act = (gate * conv(a_up, D_FF + c * FFN_COLS)) * _gelu_cdf(gate)
        act_ref[:, pl.ds(c * FFN_COLS, FFN_COLS)] = act.astype(BF16)
    half = ts // 2
    for r0 in (0, half):
        yh = y[r0:r0 + half] + _dot(act_ref[r0:r0 + half, done * FFN_COLS:], wdn_ref[done * FFN_COLS:, :])
        o_ref[0, r0:r0 + half, :] = _rms(yh, gf_ref[...]) if final_norm else yh


def _ffn(x, g_ffn, wup, conv_w, conv_b, wdn, g_final, final_norm):
    B, S, D = x.shape
    ts = TILE
    pad = BF16_ROWS
    assert S % ts == 0 and ts % pad == 0 and D_FF % FFN_COLS == 0
    nt = S // ts
    r = ts // pad
    nblk = S // pad
    in_specs = [
        pl.BlockSpec((1, ts, D), lambda b, i: (b, i, 0)),
        pl.BlockSpec((1, pad, D), lambda b, i: (b, jnp.maximum(i * r - 1, 0), 0)),
        pl.BlockSpec((1, pad, D), lambda b, i: (b, jnp.minimum((i + 1) * r, nblk - 1), 0)),
    ] + [_const_spec(a.shape) for a in (g_ffn, wup, conv_w, conv_b, wdn, g_final)]
    operands = (x, x, x, g_ffn, wup, conv_w, conv_b, wdn, g_final)
    return pl.pallas_call(
        functools.partial(_ffn_kernel, final_norm=final_norm),
        grid=(B, nt),
        in_specs=in_specs,
        out_specs=pl.BlockSpec((1, ts, D), lambda b, i: (b, i, 0)),
        out_shape=jax.ShapeDtypeStruct((B, S, D), F32),
        scratch_shapes=[
            pltpu.VMEM((ts + 2 * pad, D), BF16),
            pltpu.VMEM((ts, D_FF), BF16),
        ],
        compiler_params=pltpu.CompilerParams(
            dimension_semantics=("arbitrary", "arbitrary"), vmem_limit_bytes=VMEM_LIMIT),
        name="ffn",
    )(*operands)


def _t5_bucket(rel):
    nb = N_BUCKETS // 2
    max_exact = nb // 2
    ret = (rel > 0).astype(jnp.int32) * nb
    n = jnp.abs(rel)
    nf = jnp.maximum(n, 1).astype(jnp.float32)
    large = max_exact + (jnp.log(nf / max_exact) / math.log(MAX_DISTANCE / max_exact)
                         * (nb - max_exact)).astype(jnp.int32)
    large = jnp.minimum(large, nb - 1)
    return ret + jnp.where(n < max_exact, n, large)


def kernel(x_prompt, x_sample, mem_prompt, mem_sample, rel_bias, g_mix, w_in, attn_sink,
           sgu_w, sgu_b, sgu_ln_g, sgu_ln_b, g_mem, w_mem_kv, w_br_attn, w_br_sgu, w_br_mem,
           w_out, g_ffn, w_up, conv_w, conv_b, w_down, g_final):
    depth = w_in.shape[0]
    d = jnp.arange(REL_SPAN)
    rel = jnp.where(d < 3 * BLOCK, d, d - REL_SPAN) - BLOCK
    rel_bias_win = rel_bias[_t5_bucket(rel)].astype(F32).T
    row = lambda a: a.reshape(1, -1)

    def encoder(x, mem):
        for l in range(depth):
            col_scale = jnp.where(jnp.arange(w_in.shape[-1]) < Q_W, HEAD_DIM ** -0.5, 1.0)
            win = (w_in[l] * col_scale).astype(BF16)
            sgub = jnp.repeat(sgu_b[l].T, SGU_WIDTH // SGU_GROUPS, axis=1)
            ktm, vm = _mem_kv(mem, row(g_mem[l]), w_mem_kv[l].astype(BF16))
            x = _mixer(x, ktm, vm, rel_bias_win, attn_sink[l], row(g_mix[l]),
                       win, sgu_w[l].astype(BF16), sgub,
                       row(sgu_ln_g[l]), row(sgu_ln_b[l]),
                       w_br_attn[l].astype(BF16), w_br_sgu[l].astype(BF16),
                       w_br_mem[l].astype(BF16), w_out[l].astype(BF16))
            x = _ffn(x, row(g_ffn[l]), w_up[l].astype(BF16), conv_w[l], row(conv_b[l]),
                     w_down[l].astype(BF16), row(g_final), final_norm=(l == depth - 1))
        return x

    return encoder(x_prompt, mem_prompt), encoder(x_sample, mem_sample)
```

```python
import functools
import math

import jax
import jax.numpy as jnp
from jax import lax
from jax.experimental import pallas as pl
from jax.experimental.pallas import tpu as pltpu

D_MODEL = 1024
HEAD_DIM = 64
N_HEADS = 8
N_KV_HEADS = 2
WINDOW = 128
BLOCK = 128
N_BUCKETS = 32
MAX_DISTANCE = 128
SGU_WIDTH = 512
SGU_GROUPS = 4
SGU_CHUNK = 128
MEM_HEADS = 4
MEM_HEAD_DIM = 128
MEM_WIDTH = MEM_HEADS * MEM_HEAD_DIM
D_FF = 2816
EPS = 1e-6
NEG = -1e30

REL_SPAN = 4 * BLOCK
Q_W = N_HEADS * HEAD_DIM
KV_W = N_KV_HEADS * HEAD_DIM
OFF_K = Q_W
OFF_Z = Q_W + 2 * KV_W
OFF_QM = OFF_Z + 2 * SGU_WIDTH
OFF_GL = OFF_QM + MEM_WIDTH

LANES = 128
SUBLANES = 8
BF16_ROWS = 16
VMEM_LIMIT = 56 * 1024 * 1024

TILE = 512
FFN_COLS = 512
DOWN_FLUSH = (2, 4, 5)
ATTN_GROUP = 1
PROJ_COLS = 256

BF16 = jnp.bfloat16
F32 = jnp.float32


def _rms(x, g):
    return x * lax.rsqrt(jnp.mean(x * x, axis=-1, keepdims=True) + EPS) * g


_GELU_C0 = math.sqrt(2.0 / math.pi)
_GELU_C1 = _GELU_C0 * 0.044715


def _gelu_cdf(x):
    return 0.5 + 0.5 * jnp.tanh(x * (_GELU_C0 + _GELU_C1 * (x * x)))


def _dot(a, b):
    return jnp.dot(a, b, preferred_element_type=F32)


def _const_spec(shape):
    nd = len(shape)
    return pl.BlockSpec(shape, lambda *_: (0,) * nd, pipeline_mode=pl.Buffered(1))


def _mem_kv_kernel(mem_ref, g_ref, w_ref, kt_ref, v_ref):
    h = _rms(mem_ref[0], g_ref[...]).astype(BF16)
    kv = _dot(h, w_ref[...])
    k = kv[:, :MEM_WIDTH] * (MEM_HEAD_DIM ** -0.5)
    kt_ref[0] = k.T.astype(BF16)
    v_ref[0] = kv[:, MEM_WIDTH:].astype(BF16)


def _mem_kv(mem, g_mem, w_mem_kv):
    B, M, D = mem.shape
    return pl.pallas_call(
        _mem_kv_kernel,
        grid=(B,),
        in_specs=[
            pl.BlockSpec((1, M, D), lambda b: (b, 0, 0)),
            _const_spec((1, D)),
            _const_spec((D, 2 * MEM_WIDTH)),
        ],
        out_specs=[
            pl.BlockSpec((1, MEM_WIDTH, M), lambda b: (b, 0, 0)),
            pl.BlockSpec((1, M, MEM_WIDTH), lambda b: (b, 0, 0)),
        ],
        out_shape=[
            jax.ShapeDtypeStruct((B, MEM_WIDTH, M), BF16),
            jax.ShapeDtypeStruct((B, M, MEM_WIDTH), BF16),
        ],
        compiler_params=pltpu.CompilerParams(
            dimension_semantics=("arbitrary",), vmem_limit_bytes=VMEM_LIMIT),
        name="mem_kv",
    )(mem, g_mem, w_mem_kv)


def _mixer_kernel(x_ref, xp_ref, xn_ref, ktm_ref, vm_ref, relb_ref, sink_ref,
                  g_ref, win_ref,
                  sguw_ref, sgub_ref, lng_ref, lnb_ref,
                  wba_ref, wbs_ref, wbm_ref, wo_ref,
                  o_ref,
                  bias_ref, hext_ref, kvar_ref, vvar_ref, q_ref, attn_ref,
                  u_ref, vg_ref, vln_ref, sgu_ref, qm_ref, mo_ref, gate_ref):
    ts = x_ref.shape[1]
    nq = ts // BLOCK
    i = pl.program_id(1)
    nt = pl.num_programs(1)
    first_step = jnp.logical_and(pl.program_id(0) == 0, i == 0)

    @pl.when(first_step)
    def _():
        qi = lax.broadcasted_iota(jnp.int32, (BLOCK, 3 * BLOCK), 0)
        kj = lax.broadcasted_iota(jnp.int32, (BLOCK, 3 * BLOCK), 1)
        in_window = jnp.abs(kj - BLOCK - qi) <= WINDOW
        sub = lax.broadcasted_iota(jnp.int32, (SUBLANES, REL_SPAN), 0)
        for hd in range(N_HEADS):
            t8 = jnp.broadcast_to(relb_ref[hd:hd + 1, :], (SUBLANES, REL_SPAN))
            for bit in range(SUBLANES.bit_length() - 1):
                t8 = jnp.where((sub >> bit) & 1 == 1, pltpu.roll(t8, 1 << bit, axis=1), t8)
            t = jnp.concatenate(
                [t8] + [pltpu.roll(t8, SUBLANES * k, axis=1) for k in range(1, BLOCK // SUBLANES)],
                axis=0)
            base = jnp.where(in_window, t[:, :3 * BLOCK], NEG)
            kv, rem = divmod(hd, N_HEADS // N_KV_HEADS)
            half, slot = divmod(rem, 2)
            c = kv * 2 + slot
            rows = pl.ds(half * BLOCK, BLOCK)
            bias_ref[0, c, rows, :] = base
            bias_ref[1, c, rows, :] = jnp.where(kj < BLOCK, NEG, base)
            bias_ref[2, c, rows, :] = jnp.where(kj >= 2 * BLOCK, NEG, base)

    g = g_ref[...]
    hext_ref[0:BLOCK] = _rms(xp_ref[0], g).astype(BF16)
    hext_ref[BLOCK:BLOCK + ts] = _rms(x_ref[0], g).astype(BF16)
    hext_ref[BLOCK + ts:] = _rms(xn_ref[0], g).astype(BF16)
    h = hext_ref[BLOCK:BLOCK + ts]

    w = ts + 2 * BLOCK
    kv_all = _dot(hext_ref[...], win_ref[:, pl.ds(OFF_K, 2 * KV_W)])
    kt = kv_all[:, :KV_W].T.astype(BF16)
    v = kv_all[:, KV_W:]
    vswap = pltpu.roll(v, HEAD_DIM, axis=1)
    left = lax.broadcasted_iota(jnp.int32, (w, KV_W), 1) < HEAD_DIM
    zrow = jnp.zeros((HEAD_DIM, w), BF16)
    for kvh in range(N_KV_HEADS):
        kk = kt[kvh * HEAD_DIM:(kvh + 1) * HEAD_DIM]
        kvar_ref[kvh * 2 + 0] = jnp.concatenate([kk, zrow], axis=0)
        kvar_ref[kvh * 2 + 1] = jnp.concatenate([zrow, kk], axis=0)
    vvar_ref[0] = jnp.where(left, v, 0.0).astype(BF16)
    vvar_ref[1] = jnp.where(left, 0.0, vswap).astype(BF16)
    vvar_ref[2] = jnp.where(left, vswap, 0.0).astype(BF16)
    vvar_ref[3] = jnp.where(left, 0.0, v).astype(BF16)

    def proj_chunk(off, col0):
        return _dot(h, win_ref[:, pl.ds(off + col0, PROJ_COLS)])

    def qm_task(col0):
        qm_ref[:, pl.ds(col0, PROJ_COLS)] = proj_chunk(OFF_QM, col0).astype(BF16)

    def z_task(col0):
        z = proj_chunk(OFF_Z, col0)
        z = z * _gelu_cdf(z)
        if col0 < SGU_WIDTH:
            u_ref[:, pl.ds(col0, PROJ_COLS)] = z
        else:
            vg_ref[:, pl.ds(col0 - SGU_WIDTH, PROJ_COLS)] = z

    def gate_task(col0):
        gate_ref[:, pl.ds(col0, PROJ_COLS)] = jax.nn.sigmoid(proj_chunk(OFF_GL, col0))

    tasks = ([functools.partial(qm_task, c0) for c0 in range(0, MEM_WIDTH, PROJ_COLS)]
             + [functools.partial(z_task, c0) for c0 in range(0, 2 * SGU_WIDTH, PROJ_COLS)]
             + [functools.partial(gate_task, c0) for c0 in range(0, 3 * D_MODEL, PROJ_COLS)])

    def run_task():
        if tasks:
            tasks.pop(0)()

    q_ref[...] = _dot(h, win_ref[:, pl.ds(0, Q_W)]).astype(BF16)
    row_half = lax.broadcasted_iota(jnp.int32, (2 * BLOCK, 1), 0) < BLOCK
    def block_sel(j):
        if j == 0:
            return jnp.where(i == 0, 1, 0)
        if j == nq - 1:
            return jnp.where(i == nt - 1, 2, 0)
        return 0

    slot_no = 0
    for j0 in range(0, nq, ATTN_GROUP):
        combos = [(j, kvh, slot) for j in range(j0, j0 + ATTN_GROUP)
                  for kvh in range(N_KV_HEADS) for slot in range(2)]
        scores = []
        for j, kvh, slot in combos:
            rows = pl.ds(j * BLOCK, BLOCK)
            qs = jnp.concatenate(
                [q_ref[rows, pl.ds((2 * kvh) * LANES, LANES)],
                 q_ref[rows, pl.ds((2 * kvh + 1) * LANES, LANES)]], axis=0)
            scores.append(_dot(qs, kvar_ref[kvh * 2 + slot, :, pl.ds(j * BLOCK, 3 * BLOCK)]))
            if slot_no % 3 != 2:
                run_task()
            slot_no += 1
        probs = []
        for (j, kvh, slot), s in zip(combos, scores):
            s = s + bias_ref[block_sel(j), kvh * 2 + slot]
            snk = jnp.where(row_half, sink_ref[4 * kvh + slot], sink_ref[4 * kvh + 2 + slot])
            m = jnp.maximum(jnp.max(s, axis=-1, keepdims=True), snk)
            p = jnp.exp(s - m)
            den = jnp.sum(p, axis=-1, keepdims=True) + jnp.exp(snk - m)
            probs.append((p.astype(BF16), 1.0 / den))
        outs = [_dot(p, vvar_ref[kvh * 2 + slot, pl.ds(j * BLOCK, 3 * BLOCK), :]) * inv
                for (j, kvh, slot), (p, inv) in zip(combos, probs)]
        for g in range(ATTN_GROUP):
            pairs = []
            for kvh in range(N_KV_HEADS):
                base = (g * N_KV_HEADS + kvh) * 2
                acc = outs[base] + outs[base + 1]
                pairs += [acc[:BLOCK], acc[BLOCK:]]
            attn_ref[pl.ds((j0 + g) * BLOCK, BLOCK), :] = jnp.concatenate(pairs, axis=1).astype(BF16)

    head_cols = [pl.ds(hd * MEM_HEAD_DIM, MEM_HEAD_DIM) for hd in range(MEM_HEADS)]
    mem_scores = []
    for cols in head_cols:
        mem_scores.append(_dot(qm_ref[:, cols], ktm_ref[0, cols, :]))
        run_task()
    mem_probs = []
    for s in mem_scores:
        m = jnp.max(s, axis=-1, keepdims=True)
        p = jnp.exp(s - m)
        mem_probs.append((p.astype(BF16), 1.0 / jnp.sum(p, axis=-1, keepdims=True)))
    for cols, (p, inv) in zip(head_cols, mem_probs):
        mo_ref[:, cols] = (_dot(p, vm_ref[0, :, cols]) * inv).astype(BF16)

    vz = vg_ref[...]
    mu = jnp.mean(vz, axis=-1, keepdims=True)
    vc = vz - mu
    var = jnp.mean(vc * vc, axis=-1, keepdims=True)
    vln_ref[...] = (vc * lax.rsqrt(var + EPS) * lng_ref[...] + lnb_ref[...]).astype(BF16)
    for n in range(ts // SGU_CHUNK):
        rows = pl.ds(n * SGU_CHUNK, SGU_CHUNK)
        grp_cols = [pl.ds(grp * LANES, LANES) for grp in range(SGU_GROUPS)]
        svs = [_dot(sguw_ref[grp], vln_ref[rows, cols]) for grp, cols in enumerate(grp_cols)]
        for cols, sv in zip(grp_cols, svs):
            sgu_ref[rows, cols] = (u_ref[rows, cols] * (sv + sgub_ref[:, cols])).astype(BF16)
        run_task()
    while tasks:
        run_task()

    merged = None
    for br, (src_ref, w_ref) in enumerate(((attn_ref, wba_ref), (sgu_ref, wbs_ref), (mo_ref, wbm_ref))):
        term = gate_ref[:, pl.ds(br * D_MODEL, D_MODEL)] * _dot(src_ref[...], w_ref[...])
        merged = term if merged is None else merged + term
    o_ref[0] = x_ref[0] + _dot(merged.astype(BF16), wo_ref[...])


def _mixer(x, ktm, vm, rel_bias_win, sink, g_mix, win,
           sguw, sgub, lng, lnb, wba, wbs, wbm, wo):
    B, S, D = x.shape
    M = vm.shape[1]
    ts = TILE
    assert S % ts == 0 and ts % BLOCK == 0 and ts // BLOCK >= 2 and S // BLOCK >= 2
    nt = S // ts
    r = ts // BLOCK
    nblk = S // BLOCK
    w = ts + 2 * BLOCK
    smem = pl.BlockSpec(memory_space=pltpu.SMEM)
    in_specs = [
        pl.BlockSpec((1, ts, D), lambda b, i: (b, i, 0)),
        pl.BlockSpec((1, BLOCK, D), lambda b, i: (b, jnp.maximum(i * r - 1, 0), 0)),
        pl.BlockSpec((1, BLOCK, D), lambda b, i: (b, jnp.minimum((i + 1) * r, nblk - 1), 0)),
        pl.BlockSpec((1, MEM_WIDTH, M), lambda b, i: (b, 0, 0)),
        pl.BlockSpec((1, M, MEM_WIDTH), lambda b, i: (b, 0, 0)),
        _const_spec(rel_bias_win.shape), smem,
    ] + [_const_spec(a.shape) for a in (g_mix, win, sguw, sgub, lng, lnb,
                                        wba, wbs, wbm, wo)]
    scratch = [
        pltpu.VMEM((3, 4, 2 * BLOCK, 3 * BLOCK), F32),
        pltpu.VMEM((w, D), BF16),
        pltpu.VMEM((4, KV_W, w), BF16),
        pltpu.VMEM((4, w, KV_W), BF16),
        pltpu.VMEM((ts, Q_W), BF16),
        pltpu.VMEM((ts, Q_W), BF16),
        pltpu.VMEM((ts, SGU_WIDTH), F32),
        pltpu.VMEM((ts, SGU_WIDTH), F32),
        pltpu.VMEM((ts, SGU_WIDTH), BF16),
        pltpu.VMEM((ts, SGU_WIDTH), BF16),
        pltpu.VMEM((ts, MEM_WIDTH), BF16),
        pltpu.VMEM((ts, MEM_WIDTH), BF16),
        pltpu.VMEM((ts, 3 * D_MODEL), F32),
    ]
    operands = (x, x, x, ktm, vm, rel_bias_win, sink, g_mix, win,
                sguw, sgub, lng, lnb, wba, wbs, wbm, wo)
    return pl.pallas_call(
        _mixer_kernel,
        grid=(B, nt),
        in_specs=in_specs,
        out_specs=pl.BlockSpec((1, ts, D), lambda b, i: (b, i, 0)),
        out_shape=jax.ShapeDtypeStruct((B, S, D), F32),
        scratch_shapes=scratch,
        compiler_params=pltpu.CompilerParams(
            dimension_semantics=("arbitrary", "arbitrary"), vmem_limit_bytes=VMEM_LIMIT),
        name="mixer",
    )(*operands)


def _ffn_kernel(x_ref, xp_ref, xn_ref, g_ref, wup_ref, cw_ref, cb_ref, wdn_ref, gf_ref,
                o_ref, hext_ref, act_ref, *, final_norm):
    ts = x_ref.shape[1]
    pad = BF16_ROWS
    i = pl.program_id(1)
    nt = pl.num_programs(1)
    g = g_ref[...]
    hp = _rms(xp_ref[0], g) * jnp.where(i > 0, 1.0, 0.0)
    hn = _rms(xn_ref[0], g) * jnp.where(i < nt - 1, 1.0, 0.0)
    hext_ref[0:pad] = hp.astype(BF16)
    hext_ref[pad:pad + ts] = _rms(x_ref[0], g).astype(BF16)
    hext_ref[pad + ts:] = hn.astype(BF16)

    rows_ext = ts + 2 * pad

    def up(col0, width):
        return _dot(hext_ref[...], wup_ref[:, pl.ds(col0, width)])

    def conv(a, col0):
        cols = pl.ds(col0, a.shape[1])
        prev = pltpu.roll(a, 1, axis=0)
        nxt = pltpu.roll(a, rows_ext - 1, axis=0)
        mid = slice(pad, pad + ts)
        return (prev[mid] * cw_ref[0:1, cols] + a[mid] * cw_ref[1:2, cols]
                + nxt[mid] * cw_ref[2:3, cols] + cb_ref[:, cols])

    chunks = [(c0, min(FFN_COLS, D_FF - c0)) for c0 in range(0, D_FF, FFN_COLS)]
    y = x_ref[0]
    done = 0
    up_pair = lambda c0, wd: (up(c0, wd), up(D_FF + c0, wd))
    nxt_pair = up_pair(*chunks[0])
    for c, (c0, wd) in enumerate(chunks):
        a_gate, a_up = nxt_pair
        if c + 1 < len(chunks):
            nxt_pair = up_pair(*chunks[c + 1])
        if c in DOWN_FLUSH:
            y = y + _dot(act_ref[:, done:c0], wdn_ref[done:c0, :])
            done = c0
        gate = conv(a_gate, c0)
        act = (gate * conv(a_up, D_FF + c0)) * _gelu_cdf(gate)
        act_ref[:, pl.ds(c0, wd)] = act.astype(BF16)
    half = ts // 2
    for r0 in (0, half):
        yh = y[r0:r0 + half] + _dot(act_ref[r0:r0 + half, done:], wdn_ref[done:, :])
        o_ref[0, r0:r0 + half, :] = _rms(yh, gf_ref[...]) if final_norm else yh


def _ffn(x, g_ffn, wup, conv_w, conv_b, wdn, g_final, final_norm):
    B, S, D = x.shape
    ts = TILE
    pad = BF16_ROWS
    assert S % ts == 0 and ts % pad == 0 and FFN_COLS % LANES == 0
    nt = S // ts
    r = ts // pad
    nblk = S // pad
    in_specs = [
        pl.BlockSpec((1, ts, D), lambda b, i: (b, i, 0)),
        pl.BlockSpec((1, pad, D), lambda b, i: (b, jnp.maximum(i * r - 1, 0), 0)),
        pl.BlockSpec((1, pad, D), lambda b, i: (b, jnp.minimum((i + 1) * r, nblk - 1), 0)),
    ] + [_const_spec(a.shape) for a in (g_ffn, wup, conv_w, conv_b, wdn, g_final)]
    operands = (x, x, x, g_ffn, wup, conv_w, conv_b, wdn, g_final)
    return pl.pallas_call(
        functools.partial(_ffn_kernel, final_norm=final_norm),
        grid=(B, nt),
        in_specs=in_specs,
        out_specs=pl.BlockSpec((1, ts, D), lambda b, i: (b, i, 0)),
        out_shape=jax.ShapeDtypeStruct((B, S, D), F32),
        scratch_shapes=[
            pltpu.VMEM((ts + 2 * pad, D), BF16),
            pltpu.VMEM((ts, D_FF), BF16),
        ],
        compiler_params=pltpu.CompilerParams(
            dimension_semantics=("arbitrary", "arbitrary"), vmem_limit_bytes=VMEM_LIMIT),
        name="ffn",
    )(*operands)


def _t5_bucket(rel):
    nb = N_BUCKETS // 2
    max_exact = nb // 2
    ret = (rel > 0).astype(jnp.int32) * nb
    n = jnp.abs(rel)
    nf = jnp.maximum(n, 1).astype(jnp.float32)
    large = max_exact + (jnp.log(nf / max_exact) / math.log(MAX_DISTANCE / max_exact)
                         * (nb - max_exact)).astype(jnp.int32)
    large = jnp.minimum(large, nb - 1)
    return ret + jnp.where(n < max_exact, n, large)


def kernel(x_prompt, x_sample, mem_prompt, mem_sample, rel_bias, g_mix, w_in, attn_sink,
           sgu_w, sgu_b, sgu_ln_g, sgu_ln_b, g_mem, w_mem_kv, w_br_attn, w_br_sgu, w_br_mem,
           w_out, g_ffn, w_up, conv_w, conv_b, w_down, g_final):
    depth = w_in.shape[0]
    d = jnp.arange(REL_SPAN)
    rel = jnp.where(d < 3 * BLOCK, d, d - REL_SPAN) - BLOCK
    rel_bias_win = rel_bias[_t5_bucket(rel)].astype(F32).T
    row = lambda a: a.reshape(1, -1)

    def encoder(x, mem):
        for l in range(depth):
            col_scale = jnp.where(jnp.arange(w_in.shape[-1]) < Q_W, HEAD_DIM ** -0.5, 1.0)
            win = (w_in[l] * col_scale).astype(BF16)
            sgub = jnp.repeat(sgu_b[l].T, SGU_WIDTH // SGU_GROUPS, axis=1)
            ktm, vm = _mem_kv(mem, row(g_mem[l]), w_mem_kv[l].astype(BF16))
            x = _mixer(x, ktm, vm, rel_bias_win, attn_sink[l], row(g_mix[l]),
                       win, sgu_w[l].astype(BF16), sgub,
                       row(sgu_ln_g[l]), row(sgu_ln_b[l]),
                       w_br_attn[l].astype(BF16), w_br_sgu[l].astype(BF16),
                       w_br_mem[l].astype(BF16), w_out[l].astype(BF16))
            x = _ffn(x, row(g_ffn[l]), w_up[l].astype(BF16), conv_w[l], row(conv_b[l]),
                     w_down[l].astype(BF16), row(g_final), final_norm=(l == depth - 1))
        return x

    return encoder(x_prompt, mem_prompt), encoder(x_sample, mem_sample)
```

```python
import functools
import math

import jax
import jax.numpy as jnp
from jax import lax
from jax.experimental import pallas as pl
from jax.experimental.pallas import tpu as pltpu

D_MODEL = 1024
HEAD_DIM = 64
N_HEADS = 8
N_KV_HEADS = 2
WINDOW = 128
BLOCK = 128
N_BUCKETS = 32
MAX_DISTANCE = 128
SGU_WIDTH = 512
SGU_GROUPS = 4
SGU_CHUNK = 128
MEM_HEADS = 4
MEM_HEAD_DIM = 128
MEM_WIDTH = MEM_HEADS * MEM_HEAD_DIM
D_FF = 2816
EPS = 1e-6
NEG = -1e30

REL_SPAN = 4 * BLOCK
Q_W = N_HEADS * HEAD_DIM
KV_W = N_KV_HEADS * HEAD_DIM
OFF_K = Q_W
OFF_Z = Q_W + 2 * KV_W
OFF_QM = OFF_Z + 2 * SGU_WIDTH
OFF_GL = OFF_QM + MEM_WIDTH

LANES = 128
SUBLANES = 8
BF16_ROWS = 16
VMEM_LIMIT = 56 * 1024 * 1024

TILE = 512
FFN_COLS = 512
DOWN_FLUSH = (2, 4, 5)
ATTN_GROUP = 1
PROJ_COLS = 256

BF16 = jnp.bfloat16
F32 = jnp.float32


def _rms(x, g):
    return x * lax.rsqrt(jnp.mean(x * x, axis=-1, keepdims=True) + EPS) * g


_GELU_C0 = math.sqrt(2.0 / math.pi)
_GELU_C1 = _GELU_C0 * 0.044715


def _gelu_cdf(x):
    return 0.5 + 0.5 * jnp.tanh(x * (_GELU_C0 + _GELU_C1 * (x * x)))


def _dot(a, b):
    return jnp.dot(a, b, preferred_element_type=F32)


def _const_spec(shape):
    nd = len(shape)
    return pl.BlockSpec(shape, lambda *_: (0,) * nd, pipeline_mode=pl.Buffered(1))


def _mem_kv_kernel(mem_ref, g_ref, w_ref, kt_ref, v_ref):
    h = _rms(mem_ref[0], g_ref[...]).astype(BF16)
    kv = _dot(h, w_ref[...])
    k = kv[:, :MEM_WIDTH] * (MEM_HEAD_DIM ** -0.5)
    kt_ref[0] = k.T.astype(BF16)
    v_ref[0] = kv[:, MEM_WIDTH:].astype(BF16)


def _mem_kv(mem, g_mem, w_mem_kv):
    B, M, D = mem.shape
    return pl.pallas_call(
        _mem_kv_kernel,
        grid=(B,),
        in_specs=[
            pl.BlockSpec((1, M, D), lambda b: (b, 0, 0)),
            _const_spec((1, D)),
            _const_spec((D, 2 * MEM_WIDTH)),
        ],
        out_specs=[
            pl.BlockSpec((1, MEM_WIDTH, M), lambda b: (b, 0, 0)),
            pl.BlockSpec((1, M, MEM_WIDTH), lambda b: (b, 0, 0)),
        ],
        out_shape=[
            jax.ShapeDtypeStruct((B, MEM_WIDTH, M), BF16),
            jax.ShapeDtypeStruct((B, M, MEM_WIDTH), BF16),
        ],
        compiler_params=pltpu.CompilerParams(
            dimension_semantics=("arbitrary",), vmem_limit_bytes=VMEM_LIMIT),
        name="mem_kv",
    )(mem, g_mem, w_mem_kv)


def _mixer_kernel(x_ref, xp_ref, xn_ref, ktm_ref, vm_ref, relb_ref, sink_ref,
                  g_ref, win_ref,
                  sguw_ref, sgub_ref, lng_ref, lnb_ref,
                  wba_ref, wbs_ref, wbm_ref, wo_ref,
                  o_ref,
                  bias_ref, hext_ref, kvar_ref, vvar_ref, q_ref, attn_ref,
                  u_ref, vg_ref, vln_ref, sgu_ref, qm_ref, mo_ref, gate_ref):
    ts = x_ref.shape[1]
    nq = ts // BLOCK
    i = pl.program_id(1)
    nt = pl.num_programs(1)
    first_step = jnp.logical_and(pl.program_id(0) == 0, i == 0)

    @pl.when(first_step)
    def _():
        qi = lax.broadcasted_iota(jnp.int32, (BLOCK, 3 * BLOCK), 0)
        kj = lax.broadcasted_iota(jnp.int32, (BLOCK, 3 * BLOCK), 1)
        in_window = jnp.abs(kj - BLOCK - qi) <= WINDOW
        sub = lax.broadcasted_iota(jnp.int32, (SUBLANES, REL_SPAN), 0)
        for hd in range(N_HEADS):
            t8 = jnp.broadcast_to(relb_ref[hd:hd + 1, :], (SUBLANES, REL_SPAN))
            for bit in range(SUBLANES.bit_length() - 1):
                t8 = jnp.where((sub >> bit) & 1 == 1, pltpu.roll(t8, 1 << bit, axis=1), t8)
            t = jnp.concatenate(
                [t8] + [pltpu.roll(t8, SUBLANES * k, axis=1) for k in range(1, BLOCK // SUBLANES)],
                axis=0)
            base = jnp.where(in_window, t[:, :3 * BLOCK], NEG)
            kv, rem = divmod(hd, N_HEADS // N_KV_HEADS)
            half, slot = divmod(rem, 2)
            c = kv * 2 + slot
            rows = pl.ds(half * BLOCK, BLOCK)
            bias_ref[0, c, rows, :] = base
            bias_ref[1, c, rows, :] = jnp.where(kj < BLOCK, NEG, base)
            bias_ref[2, c, rows, :] = jnp.where(kj >= 2 * BLOCK, NEG, base)

    g = g_ref[...]
    hext_ref[0:BLOCK] = _rms(xp_ref[0], g).astype(BF16)
    hext_ref[BLOCK:BLOCK + ts] = _rms(x_ref[0], g).astype(BF16)
    hext_ref[BLOCK + ts:] = _rms(xn_ref[0], g).astype(BF16)
    h = hext_ref[BLOCK:BLOCK + ts]

    w = ts + 2 * BLOCK
    kv_all = _dot(hext_ref[...], win_ref[:, pl.ds(OFF_K, 2 * KV_W)])
    kt = kv_all[:, :KV_W].T.astype(BF16)
    v = kv_all[:, KV_W:]
    vswap = pltpu.roll(v, HEAD_DIM, axis=1)
    left = lax.broadcasted_iota(jnp.int32, (w, KV_W), 1) < HEAD_DIM
    zrow = jnp.zeros((HEAD_DIM, w), BF16)
    for kvh in range(N_KV_HEADS):
        kk = kt[kvh * HEAD_DIM:(kvh + 1) * HEAD_DIM]
        kvar_ref[kvh * 2 + 0] = jnp.concatenate([kk, zrow], axis=0)
        kvar_ref[kvh * 2 + 1] = jnp.concatenate([zrow, kk], axis=0)
    vvar_ref[0] = jnp.where(left, v, 0.0).astype(BF16)
    vvar_ref[1] = jnp.where(left, 0.0, vswap).astype(BF16)
    vvar_ref[2] = jnp.where(left, vswap, 0.0).astype(BF16)
    vvar_ref[3] = jnp.where(left, 0.0, v).astype(BF16)

    def proj_chunk(off, col0):
        return _dot(h, win_ref[:, pl.ds(off + col0, PROJ_COLS)])

    def qm_task(col0):
        qm_ref[:, pl.ds(col0, PROJ_COLS)] = proj_chunk(OFF_QM, col0).astype(BF16)

    def z_task(col0):
        z = proj_chunk(OFF_Z, col0)
        z = z * _gelu_cdf(z)
        if col0 < SGU_WIDTH:
            u_ref[:, pl.ds(col0, PROJ_COLS)] = z
        else:
            vg_ref[:, pl.ds(col0 - SGU_WIDTH, PROJ_COLS)] = z

    def gate_task(col0):
        gate_ref[:, pl.ds(col0, PROJ_COLS)] = jax.nn.sigmoid(proj_chunk(OFF_GL, col0))

    tasks = ([functools.partial(qm_task, c0) for c0 in range(0, MEM_WIDTH, PROJ_COLS)]
             + [functools.partial(z_task, c0) for c0 in range(0, 2 * SGU_WIDTH, PROJ_COLS)]
             + [functools.partial(gate_task, c0) for c0 in range(0, 3 * D_MODEL, PROJ_COLS)])

    def run_task():
        if tasks:
            tasks.pop(0)()

    q_ref[...] = _dot(h, win_ref[:, pl.ds(0, Q_W)]).astype(BF16)
    row_half = lax.broadcasted_iota(jnp.int32, (2 * BLOCK, 1), 0) < BLOCK
    def block_sel(j):
        if j == 0:
            return jnp.where(i == 0, 1, 0)
        if j == nq - 1:
            return jnp.where(i == nt - 1, 2, 0)
        return 0

    slot_no = 0
    for j0 in range(0, nq, ATTN_GROUP):
        combos = [(j, kvh, slot) for j in range(j0, j0 + ATTN_GROUP)
                  for kvh in range(N_KV_HEADS) for slot in range(2)]
        scores = []
        for j, kvh, slot in combos:
            if slot == 0:
                rows = pl.ds(j * BLOCK, BLOCK)
                keys = pl.ds(j * BLOCK, 3 * BLOCK)
                qs = jnp.concatenate(
                    [q_ref[rows, pl.ds((2 * kvh) * LANES, LANES)],
                     q_ref[rows, pl.ds((2 * kvh + 1) * LANES, LANES)]], axis=0)
                kcat = jnp.concatenate([kvar_ref[kvh * 2, :, keys], kvar_ref[kvh * 2 + 1, :, keys]], axis=1)
                both = _dot(qs, kcat)
                scores += [both[:, :3 * BLOCK], both[:, 3 * BLOCK:]]
            if slot_no % 3 != 2:
                run_task()
            slot_no += 1
        probs = []
        for (j, kvh, slot), s in zip(combos, scores):
            s = s + bias_ref[block_sel(j), kvh * 2 + slot]
            snk = jnp.where(row_half, sink_ref[4 * kvh + slot], sink_ref[4 * kvh + 2 + slot])
            m = jnp.maximum(jnp.max(s, axis=-1, keepdims=True), snk)
            p = jnp.exp(s - m)
            den = jnp.sum(p, axis=-1, keepdims=True) + jnp.exp(snk - m)
            probs.append((p.astype(BF16), 1.0 / den))
        outs = [_dot(p, vvar_ref[kvh * 2 + slot, pl.ds(j * BLOCK, 3 * BLOCK), :]) * inv
                for (j, kvh, slot), (p, inv) in zip(combos, probs)]
        for g in range(ATTN_GROUP):
            pairs = []
            for kvh in range(N_KV_HEADS):
                base = (g * N_KV_HEADS + kvh) * 2
                acc = outs[base] + outs[base + 1]
                pairs += [acc[:BLOCK], acc[BLOCK:]]
            attn_ref[pl.ds((j0 + g) * BLOCK, BLOCK), :] = jnp.concatenate(pairs, axis=1).astype(BF16)

    head_cols = [pl.ds(hd * MEM_HEAD_DIM, MEM_HEAD_DIM) for hd in range(MEM_HEADS)]
    mem_scores = []
    for cols in head_cols:
        mem_scores.append(_dot(qm_ref[:, cols], ktm_ref[0, cols, :]))
        run_task()
    mem_probs = []
    for s in mem_scores:
        m = jnp.max(s, axis=-1, keepdims=True)
        p = jnp.exp(s - m)
        mem_probs.append((p.astype(BF16), 1.0 / jnp.sum(p, axis=-1, keepdims=True)))
    for cols, (p, inv) in zip(head_cols, mem_probs):
        mo_ref[:, cols] = (_dot(p, vm_ref[0, :, cols]) * inv).astype(BF16)

    vz = vg_ref[...]
    mu = jnp.mean(vz, axis=-1, keepdims=True)
    vc = vz - mu
    var = jnp.mean(vc * vc, axis=-1, keepdims=True)
    vln_ref[...] = (vc * lax.rsqrt(var + EPS) * lng_ref[...] + lnb_ref[...]).astype(BF16)
    for n in range(ts // SGU_CHUNK):
        rows = pl.ds(n * SGU_CHUNK, SGU_CHUNK)
        grp_cols = [pl.ds(grp * LANES, LANES) for grp in range(SGU_GROUPS)]
        svs = [_dot(sguw_ref[grp], vln_ref[rows, cols]) for grp, cols in enumerate(grp_cols)]
        for cols, sv in zip(grp_cols, svs):
            sgu_ref[rows, cols] = (u_ref[rows, cols] * (sv + sgub_ref[:, cols])).astype(BF16)
        run_task()
    while tasks:
        run_task()

    merged = None
    for br, (src_ref, w_ref) in enumerate(((attn_ref, wba_ref), (sgu_ref, wbs_ref), (mo_ref, wbm_ref))):
        term = gate_ref[:, pl.ds(br * D_MODEL, D_MODEL)] * _dot(src_ref[...], w_ref[...])
        merged = term if merged is None else merged + term
    o_ref[0] = x_ref[0] + _dot(merged.astype(BF16), wo_ref[...])


def _mixer(x, ktm, vm, rel_bias_win, sink, g_mix, win,
           sguw, sgub, lng, lnb, wba, wbs, wbm, wo):
    B, S, D = x.shape
    M = vm.shape[1]
    ts = TILE
    assert S % ts == 0 and ts % BLOCK == 0 and ts // BLOCK >= 2 and S // BLOCK >= 2
    nt = S // ts
    r = ts // BLOCK
    nblk = S // BLOCK
    w = ts + 2 * BLOCK
    smem = pl.BlockSpec(memory_space=pltpu.SMEM)
    in_specs = [
        pl.BlockSpec((1, ts, D), lambda b, i: (b, i, 0)),
        pl.BlockSpec((1, BLOCK, D), lambda b, i: (b, jnp.maximum(i * r - 1, 0), 0)),
        pl.BlockSpec((1, BLOCK, D), lambda b, i: (b, jnp.minimum((i + 1) * r, nblk - 1), 0)),
        pl.BlockSpec((1, MEM_WIDTH, M), lambda b, i: (b, 0, 0)),
        pl.BlockSpec((1, M, MEM_WIDTH), lambda b, i: (b, 0, 0)),
        _const_spec(rel_bias_win.shape), smem,
    ] + [_const_spec(a.shape) for a in (g_mix, win, sguw, sgub, lng, lnb,
                                        wba, wbs, wbm, wo)]
    scratch = [
        pltpu.VMEM((3, 4, 2 * BLOCK, 3 * BLOCK), F32),
        pltpu.VMEM((w, D), BF16),
        pltpu.VMEM((4, KV_W, w), BF16),
        pltpu.VMEM((4, w, KV_W), BF16),
        pltpu.VMEM((ts, Q_W), BF16),
        pltpu.VMEM((ts, Q_W), BF16),
        pltpu.VMEM((ts, SGU_WIDTH), F32),
        pltpu.VMEM((ts, SGU_WIDTH), F32),
        pltpu.VMEM((ts, SGU_WIDTH), BF16),
        pltpu.VMEM((ts, SGU_WIDTH), BF16),
        pltpu.VMEM((ts, MEM_WIDTH), BF16),
        pltpu.VMEM((ts, MEM_WIDTH), BF16),
        pltpu.VMEM((ts, 3 * D_MODEL), F32),
    ]
    operands = (x, x, x, ktm, vm, rel_bias_win, sink, g_mix, win,
                sguw, sgub, lng, lnb, wba, wbs, wbm, wo)
    return pl.pallas_call(
        _mixer_kernel,
        grid=(B, nt),
        in_specs=in_specs,
        out_specs=pl.BlockSpec((1, ts, D), lambda b, i: (b, i, 0)),
        out_shape=jax.ShapeDtypeStruct((B, S, D), F32),
        scratch_shapes=scratch,
        compiler_params=pltpu.CompilerParams(
            dimension_semantics=("arbitrary", "arbitrary"), vmem_limit_bytes=VMEM_LIMIT),
        name="mixer",
    )(*operands)


def _ffn_kernel(x_ref, xp_ref, xn_ref, g_ref, wup_ref, cw_ref, cb_ref, wdn_ref, gf_ref,
                o_ref, hext_ref, act_ref, *, final_norm):
    ts = x_ref.shape[1]
    pad = BF16_ROWS
    i = pl.program_id(1)
    nt = pl.num_programs(1)
    g = g_ref[...]
    hp = _rms(xp_ref[0], g) * jnp.where(i > 0, 1.0, 0.0)
    hn = _rms(xn_ref[0], g) * jnp.where(i < nt - 1, 1.0, 0.0)
    hext_ref[0:pad] = hp.astype(BF16)
    hext_ref[pad:pad + ts] = _rms(x_ref[0], g).astype(BF16)
    hext_ref[pad + ts:] = hn.astype(BF16)

    rows_ext = ts + 2 * pad

    def up(col0, width):
        return _dot(hext_ref[...], wup_ref[:, pl.ds(col0, width)])

    def conv(a, col0):
        cols = pl.ds(col0, a.shape[1])
        prev = pltpu.roll(a, 1, axis=0)
        nxt = pltpu.roll(a, rows_ext - 1, axis=0)
        mid = slice(pad, pad + ts)
        return (prev[mid] * cw_ref[0:1, cols] + a[mid] * cw_ref[1:2, cols]
                + nxt[mid] * cw_ref[2:3, cols] + cb_ref[:, cols])

    chunks = [(c0, min(FFN_COLS, D_FF - c0)) for c0 in range(0, D_FF, FFN_COLS)]
    y = x_ref[0]
    done = 0
    up_pair = lambda c0, wd: (up(c0, wd), up(D_FF + c0, wd))
    nxt_pair = up_pair(*chunks[0])
    for c, (c0, wd) in enumerate(chunks):
        a_gate, a_up = nxt_pair
        if c + 1 < len(chunks):
            nxt_pair = up_pair(*chunks[c + 1])
        if c in DOWN_FLUSH:
            y = y + _dot(act_ref[:, done:c0], wdn_ref[done:c0, :])
            done = c0
        gate = conv(a_gate, c0)
        act = (gate * conv(a_up, D_FF + c0)) * _gelu_cdf(gate)
        act_ref[:, pl.ds(c0, wd)] = act.astype(BF16)
    half = ts // 2
    for r0 in (0, half):
        yh = y[r0:r0 + half] + _dot(act_ref[r0:r0 + half, done:], wdn_ref[done:, :])
        o_ref[0, r0:r0 + half, :] = _rms(yh, gf_ref[...]) if final_norm else yh


def _ffn(x, g_ffn, wup, conv_w, conv_b, wdn, g_final, final_norm):
    B, S, D = x.shape
    ts = TILE
    pad = BF16_ROWS
    assert S % ts == 0 and ts % pad == 0 and FFN_COLS % LANES == 0
    nt = S // ts
    r = ts // pad
    nblk = S // pad
    in_specs = [
        pl.BlockSpec((1, ts, D), lambda b, i: (b, i, 0)),
        pl.BlockSpec((1, pad, D), lambda b, i: (b, jnp.maximum(i * r - 1, 0), 0)),
        pl.BlockSpec((1, pad, D), lambda b, i: (b, jnp.minimum((i + 1) * r, nblk - 1), 0)),
    ] + [_const_spec(a.shape) for a in (g_ffn, wup, conv_w, conv_b, wdn, g_final)]
    operands = (x, x, x, g_ffn, wup, conv_w, conv_b, wdn, g_final)
    return pl.pallas_call(
        functools.partial(_ffn_kernel, final_norm=final_norm),
        grid=(B, nt),
        in_specs=in_specs,
        out_specs=pl.BlockSpec((1, ts, D), lambda b, i: (b, i, 0)),
        out_shape=jax.ShapeDtypeStruct((B, S, D), F32),
        scratch_shapes=[
            pltpu.VMEM((ts + 2 * pad, D), BF16),
            pltpu.VMEM((ts, D_FF), BF16),
        ],
        compiler_params=pltpu.CompilerParams(
            dimension_semantics=("arbitrary", "arbitrary"), vmem_limit_bytes=VMEM_LIMIT),
        name="ffn",
    )(*operands)


def _t5_bucket(rel):
    nb = N_BUCKETS // 2
    max_exact = nb // 2
    ret = (rel > 0).astype(jnp.int32) * nb
    n = jnp.abs(rel)
    nf = jnp.maximum(n, 1).astype(jnp.float32)
    large = max_exact + (jnp.log(nf / max_exact) / math.log(MAX_DISTANCE / max_exact)
                         * (nb - max_exact)).astype(jnp.int32)
    large = jnp.minimum(large, nb - 1)
    return ret + jnp.where(n < max_exact, n, large)


def kernel(x_prompt, x_sample, mem_prompt, mem_sample, rel_bias, g_mix, w_in, attn_sink,
           sgu_w, sgu_b, sgu_ln_g, sgu_ln_b, g_mem, w_mem_kv, w_br_attn, w_br_sgu, w_br_mem,
           w_out, g_ffn, w_up, conv_w, conv_b, w_down, g_final):
    depth = w_in.shape[0]
    d = jnp.arange(REL_SPAN)
    rel = jnp.where(d < 3 * BLOCK, d, d - REL_SPAN) - BLOCK
    rel_bias_win = rel_bias[_t5_bucket(rel)].astype(F32).T
    row = lambda a: a.reshape(1, -1)

    def encoder(x, mem):
        for l in range(depth):
            col_scale = jnp.where(jnp.arange(w_in.shape[-1]) < Q_W, HEAD_DIM ** -0.5, 1.0)
            win = (w_in[l] * col_scale).astype(BF16)
            sgub = jnp.repeat(sgu_b[l].T, SGU_WIDTH // SGU_GROUPS, axis=1)
            ktm, vm = _mem_kv(mem, row(g_mem[l]), w_mem_kv[l].astype(BF16))
            x = _mixer(x, ktm, vm, rel_bias_win, attn_sink[l], row(g_mix[l]),
                       win, sgu_w[l].astype(BF16), sgub,
                       row(sgu_ln_g[l]), row(sgu_ln_b[l]),
                       w_br_attn[l].astype(BF16), w_br_sgu[l].astype(BF16),
                       w_br_mem[l].astype(BF16), w_out[l].astype(BF16))
            x = _ffn(x, row(g_ffn[l]), w_up[l].astype(BF16), conv_w[l], row(conv_b[l]),
                     w_down[l].astype(BF16), row(g_final), final_norm=(l == depth - 1))
        return x

    return encoder(x_prompt, mem_prompt), encoder(x_sample, mem_sample)
```

```python
import functools
import math

import jax
import jax.numpy as jnp
from jax import lax
from jax.experimental import pallas as pl
from jax.experimental.pallas import tpu as pltpu

D_MODEL = 1024
HEAD_DIM = 64
N_HEADS = 8
N_KV_HEADS = 2
WINDOW = 128
BLOCK = 128
N_BUCKETS = 32
MAX_DISTANCE = 128
SGU_WIDTH = 512
SGU_GROUPS = 4
SGU_CHUNK = 128
MEM_HEADS = 4
MEM_HEAD_DIM = 128
MEM_WIDTH = MEM_HEADS * MEM_HEAD_DIM
D_FF = 2816
EPS = 1e-6
NEG = -1e30

REL_SPAN = 4 * BLOCK
Q_W = N_HEADS * HEAD_DIM
KV_W = N_KV_HEADS * HEAD_DIM
OFF_K = Q_W
OFF_Z = Q_W + 2 * KV_W
OFF_QM = OFF_Z + 2 * SGU_WIDTH
OFF_GL = OFF_QM + MEM_WIDTH

LANES = 128
SUBLANES = 8
BF16_ROWS = 16
VMEM_LIMIT = 56 * 1024 * 1024

TILE = 512
FFN_COLS = 512
DOWN_FLUSH = (2, 4, 5)
ATTN_GROUP = 1
PROJ_COLS = 256

BF16 = jnp.bfloat16
F32 = jnp.float32


def _rms(x, g):
    return x * lax.rsqrt(jnp.mean(x * x, axis=-1, keepdims=True) + EPS) * g


_GELU_C0 = math.sqrt(2.0 / math.pi)
_GELU_C1 = _GELU_C0 * 0.044715


def _gelu_cdf(x):
    return 0.5 + 0.5 * jnp.tanh(x * (_GELU_C0 + _GELU_C1 * (x * x)))


def _dot(a, b):
    return jnp.dot(a, b, preferred_element_type=F32)


def _const_spec(shape):
    nd = len(shape)
    return pl.BlockSpec(shape, lambda *_: (0,) * nd, pipeline_mode=pl.Buffered(1))


def _mem_kv_kernel(mem_ref, g_ref, w_ref, kt_ref, v_ref):
    h = _rms(mem_ref[0], g_ref[...]).astype(BF16)
    kv = _dot(h, w_ref[...])
    k = kv[:, :MEM_WIDTH] * (MEM_HEAD_DIM ** -0.5)
    kt_ref[0] = k.T.astype(BF16)
    v = kv[:, MEM_WIDTH:].astype(BF16)
    ones = jnp.ones((v.shape[0], MEM_HEAD_DIM), BF16)
    v_ref[0] = jnp.concatenate(
        [blk for hd in range(MEM_HEADS)
         for blk in (v[:, hd * MEM_HEAD_DIM:(hd + 1) * MEM_HEAD_DIM], ones)], axis=1)


def _mem_kv(mem, g_mem, w_mem_kv):
    B, M, D = mem.shape
    return pl.pallas_call(
        _mem_kv_kernel,
        grid=(B,),
        in_specs=[
            pl.BlockSpec((1, M, D), lambda b: (b, 0, 0)),
            _const_spec((1, D)),
            _const_spec((D, 2 * MEM_WIDTH)),
        ],
        out_specs=[
            pl.BlockSpec((1, MEM_WIDTH, M), lambda b: (b, 0, 0)),
            pl.BlockSpec((1, M, 2 * MEM_WIDTH), lambda b: (b, 0, 0)),
        ],
        out_shape=[
            jax.ShapeDtypeStruct((B, MEM_WIDTH, M), BF16),
            jax.ShapeDtypeStruct((B, M, 2 * MEM_WIDTH), BF16),
        ],
        compiler_params=pltpu.CompilerParams(
            dimension_semantics=("arbitrary",), vmem_limit_bytes=VMEM_LIMIT),
        name="mem_kv",
    )(mem, g_mem, w_mem_kv)


def _mixer_kernel(x_ref, xp_ref, xn_ref, ktm_ref, vm_ref, relb_ref, sink_ref,
                  g_ref, win_ref,
                  sguw_ref, sgub_ref, lng_ref, lnb_ref,
                  wba_ref, wbs_ref, wbm_ref, wo_ref,
                  o_ref,
                  bias_ref, hext_ref, kvar_ref, vvar_ref, q_ref, attn_ref,
                  u_ref, vg_ref, vln_ref, sgu_ref, qm_ref, mo_ref, gate_ref):
    ts = x_ref.shape[1]
    nq = ts // BLOCK
    i = pl.program_id(1)
    nt = pl.num_programs(1)
    first_step = jnp.logical_and(pl.program_id(0) == 0, i == 0)

    @pl.when(first_step)
    def _():
        qi = lax.broadcasted_iota(jnp.int32, (BLOCK, 3 * BLOCK), 0)
        kj = lax.broadcasted_iota(jnp.int32, (BLOCK, 3 * BLOCK), 1)
        in_window = jnp.abs(kj - BLOCK - qi) <= WINDOW
        sub = lax.broadcasted_iota(jnp.int32, (SUBLANES, REL_SPAN), 0)
        for hd in range(N_HEADS):
            t8 = jnp.broadcast_to(relb_ref[hd:hd + 1, :], (SUBLANES, REL_SPAN))
            for bit in range(SUBLANES.bit_length() - 1):
                t8 = jnp.where((sub >> bit) & 1 == 1, pltpu.roll(t8, 1 << bit, axis=1), t8)
            t = jnp.concatenate(
                [t8] + [pltpu.roll(t8, SUBLANES * k, axis=1) for k in range(1, BLOCK // SUBLANES)],
                axis=0)
            base = jnp.where(in_window, t[:, :3 * BLOCK], NEG)
            kv, rem = divmod(hd, N_HEADS // N_KV_HEADS)
            half, slot = divmod(rem, 2)
            c = kv * 2 + slot
            rows = pl.ds(half * BLOCK, BLOCK)
            bias_ref[0, c, rows, :] = base
            bias_ref[1, c, rows, :] = jnp.where(kj < BLOCK, NEG, base)
            bias_ref[2, c, rows, :] = jnp.where(kj >= 2 * BLOCK, NEG, base)

    g = g_ref[...]
    hext_ref[0:BLOCK] = _rms(xp_ref[0], g).astype(BF16)
    hext_ref[BLOCK:BLOCK + ts] = _rms(x_ref[0], g).astype(BF16)
    hext_ref[BLOCK + ts:] = _rms(xn_ref[0], g).astype(BF16)
    h = hext_ref[BLOCK:BLOCK + ts]

    w = ts + 2 * BLOCK
    kv_all = _dot(hext_ref[...], win_ref[:, pl.ds(OFF_K, 2 * KV_W)])
    kt = kv_all[:, :KV_W].T.astype(BF16)
    v = kv_all[:, KV_W:]
    vswap = pltpu.roll(v, HEAD_DIM, axis=1)
    left = lax.broadcasted_iota(jnp.int32, (w, KV_W), 1) < HEAD_DIM
    zrow = jnp.zeros((HEAD_DIM, w), BF16)
    for kvh in range(N_KV_HEADS):
        kk = kt[kvh * HEAD_DIM:(kvh + 1) * HEAD_DIM]
        kvar_ref[kvh * 2 + 0] = jnp.concatenate([kk, zrow], axis=0)
        kvar_ref[kvh * 2 + 1] = jnp.concatenate([zrow, kk], axis=0)
    vvar_ref[0] = jnp.where(left, v, 0.0).astype(BF16)
    vvar_ref[1] = jnp.where(left, 0.0, vswap).astype(BF16)
    vvar_ref[2] = jnp.where(left, vswap, 0.0).astype(BF16)
    vvar_ref[3] = jnp.where(left, 0.0, v).astype(BF16)

    def proj_chunk(off, col0):
        return _dot(h, win_ref[:, pl.ds(off + col0, PROJ_COLS)])

    def qm_task(col0):
        qm_ref[:, pl.ds(col0, PROJ_COLS)] = proj_chunk(OFF_QM, col0).astype(BF16)

    def z_task(col0):
        z = proj_chunk(OFF_Z, col0)
        z = z * _gelu_cdf(z)
        if col0 < SGU_WIDTH:
            u_ref[:, pl.ds(col0, PROJ_COLS)] = z
        else:
            vg_ref[:, pl.ds(col0 - SGU_WIDTH, PROJ_COLS)] = z

    def gate_task(col0):
        gate_ref[:, pl.ds(col0, PROJ_COLS)] = jax.nn.sigmoid(proj_chunk(OFF_GL, col0))

    tasks = ([functools.partial(qm_task, c0) for c0 in range(0, MEM_WIDTH, PROJ_COLS)]
             + [functools.partial(z_task, c0) for c0 in range(0, 2 * SGU_WIDTH, PROJ_COLS)]
             + [functools.partial(gate_task, c0) for c0 in range(0, 3 * D_MODEL, PROJ_COLS)])

    def run_task():
        if tasks:
            tasks.pop(0)()

    q_ref[...] = _dot(h, win_ref[:, pl.ds(0, Q_W)]).astype(BF16)
    row_half = lax.broadcasted_iota(jnp.int32, (2 * BLOCK, 1), 0) < BLOCK
    def block_sel(j):
        if j == 0:
            return jnp.where(i == 0, 1, 0)
        if j == nq - 1:
            return jnp.where(i == nt - 1, 2, 0)
        return 0

    slot_no = 0
    for j0 in range(0, nq, ATTN_GROUP):
        combos = [(j, kvh, slot) for j in range(j0, j0 + ATTN_GROUP)
                  for kvh in range(N_KV_HEADS) for slot in range(2)]
        scores = []
        for j, kvh, slot in combos:
            if slot == 0:
                rows = pl.ds(j * BLOCK, BLOCK)
                keys = pl.ds(j * BLOCK, 3 * BLOCK)
                qs = jnp.concatenate(
                    [q_ref[rows, pl.ds((2 * kvh) * LANES, LANES)],
                     q_ref[rows, pl.ds((2 * kvh + 1) * LANES, LANES)]], axis=0)
                kcat = jnp.concatenate([kvar_ref[kvh * 2, :, keys], kvar_ref[kvh * 2 + 1, :, keys]], axis=1)
                both = _dot(qs, kcat)
                scores += [both[:, :3 * BLOCK], both[:, 3 * BLOCK:]]
            if slot_no % 2 == 0:
                run_task()
            slot_no += 1
        probs = []
        for (j, kvh, slot), s in zip(combos, scores):
            s = s + bias_ref[block_sel(j), kvh * 2 + slot]
            snk = jnp.where(row_half, sink_ref[4 * kvh + slot], sink_ref[4 * kvh + 2 + slot])
            m = jnp.maximum(jnp.max(s, axis=-1, keepdims=True), snk)
            p = jnp.exp(s - m)
            den = jnp.sum(p, axis=-1, keepdims=True) + jnp.exp(snk - m)
            probs.append((p.astype(BF16), 1.0 / den))
        outs = [_dot(p, vvar_ref[kvh * 2 + slot, pl.ds(j * BLOCK, 3 * BLOCK), :]) * inv
                for (j, kvh, slot), (p, inv) in zip(combos, probs)]
        for g in range(ATTN_GROUP):
            pairs = []
            for kvh in range(N_KV_HEADS):
                base = (g * N_KV_HEADS + kvh) * 2
                acc = outs[base] + outs[base + 1]
                pairs += [acc[:BLOCK], acc[BLOCK:]]
            attn_ref[pl.ds((j0 + g) * BLOCK, BLOCK), :] = jnp.concatenate(pairs, axis=1).astype(BF16)

    head_cols = [pl.ds(hd * MEM_HEAD_DIM, MEM_HEAD_DIM) for hd in range(MEM_HEADS)]
    mem_scores = []
    for cols in head_cols:
        mem_scores.append(_dot(qm_ref[:, cols], ktm_ref[0, cols, :]))
        run_task()
    mem_probs = []
    for s in mem_scores:
        m = jnp.max(s, axis=-1, keepdims=True)
        mem_probs.append(jnp.exp(s - m).astype(BF16))
    for hd, (cols, p) in enumerate(zip(head_cols, mem_probs)):
        pv = _dot(p, vm_ref[0, :, pl.ds(hd * 2 * MEM_HEAD_DIM, 2 * MEM_HEAD_DIM)])
        mo_ref[:, cols] = (pv[:, :MEM_HEAD_DIM] * (1.0 / pv[:, MEM_HEAD_DIM:])).astype(BF16)

    vz = vg_ref[...]
    mu = jnp.mean(vz, axis=-1, keepdims=True)
    vc = vz - mu
    var = jnp.mean(vc * vc, axis=-1, keepdims=True)
    vln_ref[...] = (vc * lax.rsqrt(var + EPS) * lng_ref[...] + lnb_ref[...]).astype(BF16)
    for n in range(ts // SGU_CHUNK):
        rows = pl.ds(n * SGU_CHUNK, SGU_CHUNK)
        grp_cols = [pl.ds(grp * LANES, LANES) for grp in range(SGU_GROUPS)]
        svs = [_dot(sguw_ref[grp], vln_ref[rows, cols]) for grp, cols in enumerate(grp_cols)]
        for cols, sv in zip(grp_cols, svs):
            sgu_ref[rows, cols] = (u_ref[rows, cols] * (sv + sgub_ref[:, cols])).astype(BF16)
        run_task()
    while tasks:
        run_task()

    merged = None
    for br, (src_ref, w_ref) in enumerate(((attn_ref, wba_ref), (sgu_ref, wbs_ref), (mo_ref, wbm_ref))):
        term = gate_ref[:, pl.ds(br * D_MODEL, D_MODEL)] * _dot(src_ref[...], w_ref[...])
        merged = term if merged is None else merged + term
    o_ref[0] = x_ref[0] + _dot(merged.astype(BF16), wo_ref[...])


def _mixer(x, ktm, vm, rel_bias_win, sink, g_mix, win,
           sguw, sgub, lng, lnb, wba, wbs, wbm, wo):
    B, S, D = x.shape
    M = vm.shape[1]
    ts = TILE
    assert S % ts == 0 and ts % BLOCK == 0 and ts // BLOCK >= 2 and S // BLOCK >= 2
    nt = S // ts
    r = ts // BLOCK
    nblk = S // BLOCK
    w = ts + 2 * BLOCK
    smem = pl.BlockSpec(memory_space=pltpu.SMEM)
    in_specs = [
        pl.BlockSpec((1, ts, D), lambda b, i: (b, i, 0)),
        pl.BlockSpec((1, BLOCK, D), lambda b, i: (b, jnp.maximum(i * r - 1, 0), 0)),
        pl.BlockSpec((1, BLOCK, D), lambda b, i: (b, jnp.minimum((i + 1) * r, nblk - 1), 0)),
        pl.BlockSpec((1, MEM_WIDTH, M), lambda b, i: (b, 0, 0)),
        pl.BlockSpec((1, M, 2 * MEM_WIDTH), lambda b, i: (b, 0, 0)),
        _const_spec(rel_bias_win.shape), smem,
    ] + [_const_spec(a.shape) for a in (g_mix, win, sguw, sgub, lng, lnb,
                                        wba, wbs, wbm, wo)]
    scratch = [
        pltpu.VMEM((3, 4, 2 * BLOCK, 3 * BLOCK), F32),
        pltpu.VMEM((w, D), BF16),
        pltpu.VMEM((4, KV_W, w), BF16),
        pltpu.VMEM((4, w, KV_W), BF16),
        pltpu.VMEM((ts, Q_W), BF16),
        pltpu.VMEM((ts, Q_W), BF16),
        pltpu.VMEM((ts, SGU_WIDTH), F32),
        pltpu.VMEM((ts, SGU_WIDTH), F32),
        pltpu.VMEM((ts, SGU_WIDTH), BF16),
        pltpu.VMEM((ts, SGU_WIDTH), BF16),
        pltpu.VMEM((ts, MEM_WIDTH), BF16),
        pltpu.VMEM((ts, MEM_WIDTH), BF16),
        pltpu.VMEM((ts, 3 * D_MODEL), F32),
    ]
    operands = (x, x, x, ktm, vm, rel_bias_win, sink, g_mix, win,
                sguw, sgub, lng, lnb, wba, wbs, wbm, wo)
    return pl.pallas_call(
        _mixer_kernel,
        grid=(B, nt),
        in_specs=in_specs,
        out_specs=pl.BlockSpec((1, ts, D), lambda b, i: (b, i, 0)),
        out_shape=jax.ShapeDtypeStruct((B, S, D), F32),
        scratch_shapes=scratch,
        compiler_params=pltpu.CompilerParams(
            dimension_semantics=("arbitrary", "arbitrary"), vmem_limit_bytes=VMEM_LIMIT),
        name="mixer",
    )(*operands)


def _ffn_kernel(x_ref, xp_ref, xn_ref, g_ref, wup_ref, cw_ref, cb_ref, wdn_ref, gf_ref,
                o_ref, hext_ref, act_ref, *, final_norm):
    ts = x_ref.shape[1]
    pad = BF16_ROWS
    i = pl.program_id(1)
    nt = pl.num_programs(1)
    g = g_ref[...]
    hp = _rms(xp_ref[0], g) * jnp.where(i > 0, 1.0, 0.0)
    hn = _rms(xn_ref[0], g) * jnp.where(i < nt - 1, 1.0, 0.0)
    hext_ref[0:pad] = hp.astype(BF16)
    hext_ref[pad:pad + ts] = _rms(x_ref[0], g).astype(BF16)
    hext_ref[pad + ts:] = hn.astype(BF16)

    rows_ext = ts + 2 * pad

    def up(col0, width):
        return _dot(hext_ref[...], wup_ref[:, pl.ds(col0, width)])

    def conv(a, col0):
        cols = pl.ds(col0, a.shape[1])
        prev = pltpu.roll(a, 1, axis=0)
        nxt = pltpu.roll(a, rows_ext - 1, axis=0)
        mid = slice(pad, pad + ts)
        return (prev[mid] * cw_ref[0:1, cols] + a[mid] * cw_ref[1:2, cols]
                + nxt[mid] * cw_ref[2:3, cols] + cb_ref[:, cols])

    chunks = [(c0, min(FFN_COLS, D_FF - c0)) for c0 in range(0, D_FF, FFN_COLS)]
    y = x_ref[0]
    done = 0
    up_pair = lambda c0, wd: (up(c0, wd), up(D_FF + c0, wd))
    nxt_pair = up_pair(*chunks[0])
    for c, (c0, wd) in enumerate(chunks):
        a_gate, a_up = nxt_pair
        if c + 1 < len(chunks):
            nxt_pair = up_pair(*chunks[c + 1])
        if c in DOWN_FLUSH:
            y = y + _dot(act_ref[:, done:c0], wdn_ref[done:c0, :])
            done = c0
        gate = conv(a_gate, c0)
        act = (gate * conv(a_up, D_FF + c0)) * _gelu_cdf(gate)
        act_ref[:, pl.ds(c0, wd)] = act.astype(BF16)
    half = ts // 2
    for r0 in (0, half):
        yh = y[r0:r0 + half] + _dot(act_ref[r0:r0 + half, done:], wdn_ref[done:, :])
        o_ref[0, r0:r0 + half, :] = _rms(yh, gf_ref[...]) if final_norm else yh


def _ffn(x, g_ffn, wup, conv_w, conv_b, wdn, g_final, final_norm):
    B, S, D = x.shape
    ts = TILE
    pad = BF16_ROWS
    assert S % ts == 0 and ts % pad == 0 and FFN_COLS % LANES == 0
    nt = S // ts
    r = ts // pad
    nblk = S // pad
    in_specs = [
        pl.BlockSpec((1, ts, D), lambda b, i: (b, i, 0)),
        pl.BlockSpec((1, pad, D), lambda b, i: (b, jnp.maximum(i * r - 1, 0), 0)),
        pl.BlockSpec((1, pad, D), lambda b, i: (b, jnp.minimum((i + 1) * r, nblk - 1), 0)),
    ] + [_const_spec(a.shape) for a in (g_ffn, wup, conv_w, conv_b, wdn, g_final)]
    operands = (x, x, x, g_ffn, wup, conv_w, conv_b, wdn, g_final)
    return pl.pallas_call(
        functools.partial(_ffn_kernel, final_norm=final_norm),
        grid=(B, nt),
        in_specs=in_specs,
        out_specs=pl.BlockSpec((1, ts, D), lambda b, i: (b, i, 0)),
        out_shape=jax.ShapeDtypeStruct((B, S, D), F32),
        scratch_shapes=[
            pltpu.VMEM((ts + 2 * pad, D), BF16),
            pltpu.VMEM((ts, D_FF), BF16),
        ],
        compiler_params=pltpu.CompilerParams(
            dimension_semantics=("arbitrary", "arbitrary"), vmem_limit_bytes=VMEM_LIMIT),
        name="ffn",
    )(*operands)


def _t5_bucket(rel):
    nb = N_BUCKETS // 2
    max_exact = nb // 2
    ret = (rel > 0).astype(jnp.int32) * nb
    n = jnp.abs(rel)
    nf = jnp.maximum(n, 1).astype(jnp.float32)
    large = max_exact + (jnp.log(nf / max_exact) / math.log(MAX_DISTANCE / max_exact)
                         * (nb - max_exact)).astype(jnp.int32)
    large = jnp.minimum(large, nb - 1)
    return ret + jnp.where(n < max_exact, n, large)


def kernel(x_prompt, x_sample, mem_prompt, mem_sample, rel_bias, g_mix, w_in, attn_sink,
           sgu_w, sgu_b, sgu_ln_g, sgu_ln_b, g_mem, w_mem_kv, w_br_attn, w_br_sgu, w_br_mem,
           w_out, g_ffn, w_up, conv_w, conv_b, w_down, g_final):
    depth = w_in.shape[0]
    d = jnp.arange(REL_SPAN)
    rel = jnp.where(d < 3 * BLOCK, d, d - REL_SPAN) - BLOCK
    rel_bias_win = rel_bias[_t5_bucket(rel)].astype(F32).T
    row = lambda a: a.reshape(1, -1)

    def encoder(x, mem):
        for l in range(depth):
            col_scale = jnp.where(jnp.arange(w_in.shape[-1]) < Q_W, HEAD_DIM ** -0.5, 1.0)
            win = (w_in[l] * col_scale).astype(BF16)
            sgub = jnp.repeat(sgu_b[l].T, SGU_WIDTH // SGU_GROUPS, axis=1)
            ktm, vm = _mem_kv(mem, row(g_mem[l]), w_mem_kv[l].astype(BF16))
            x = _mixer(x, ktm, vm, rel_bias_win, attn_sink[l], row(g_mix[l]),
                       win, sgu_w[l].astype(BF16), sgub,
                       row(sgu_ln_g[l]), row(sgu_ln_b[l]),
                       w_br_attn[l].astype(BF16), w_br_sgu[l].astype(BF16),
                       w_br_mem[l].astype(BF16), w_out[l].astype(BF16))
            x = _ffn(x, row(g_ffn[l]), w_up[l].astype(BF16), conv_w[l], row(conv_b[l]),
                     w_down[l].astype(BF16), row(g_final), final_norm=(l == depth - 1))
        return x

    return encoder(x_prompt, mem_prompt), encoder(x_sample, mem_sample)
```
